```python
import math
import jax, jax.numpy as jnp
from jax import lax
import numpy as np

D_MODEL = 1024
BATCH = 32
SEQ = 2048
DEPTH = 4

GRID_W = 64
Q_BLOCK = 128
EPS = 1e-6
MIN_FORGET = 1e-20

A_HEADS = 6
A_KV_HEADS = 2
A_GROUP = A_HEADS // A_KV_HEADS
A_HEAD_DIM = 64
A_AXIS_DIM = A_HEAD_DIM // 2
ROPE_THETA = 10000.0
A_WIDTH = A_HEADS * A_HEAD_DIM
A_KV_WIDTH = A_KV_HEADS * A_HEAD_DIM

B_HEADS = 4
B_HEAD_DIM = 48
B_V_DIM = 2 * B_HEAD_DIM
B_QK_WIDTH = B_HEADS * 2 * B_HEAD_DIM
B_WIDTH = B_HEADS * B_V_DIM

C_HEADS = 4
C_KEY_DIM = 64
C_VAL_DIM = 64
C_KEY_WIDTH = C_HEADS * C_KEY_DIM
C_VAL_WIDTH = C_HEADS * C_VAL_DIM
C_CHUNK = 16

D_MIX = A_WIDTH + B_WIDTH + C_VAL_WIDTH

IN_SPLITS = (A_WIDTH, A_KV_WIDTH, A_KV_WIDTH,
             B_QK_WIDTH, B_QK_WIDTH, B_WIDTH,
             C_KEY_WIDTH, C_KEY_WIDTH, C_KEY_WIDTH, C_VAL_WIDTH, C_VAL_WIDTH)
IN_COLS = A_WIDTH + 2 * A_KV_WIDTH + 2 * B_QK_WIDTH + B_WIDTH + 3 * C_KEY_WIDTH + 2 * C_VAL_WIDTH

D_FF = 2816
CONV_W = 3

kernel_name = 'hymba_style_bidir_hybrid_trunk'


def rms_norm(x, gain):
    xf = x.astype(jnp.float32)
    y = xf * lax.rsqrt(jnp.mean(xf * xf, axis=-1, keepdims=True) + EPS)
    return (y * gain.astype(jnp.float32)).astype(x.dtype)


def split_columns(proj):
    parts, start = [], 0
    for w in IN_SPLITS:
        parts.append(proj[..., start:start + w])
        start += w
    return parts


def axial_rope_tables(seq):
    rows = seq // GRID_W
    row = jnp.repeat(jnp.arange(rows), GRID_W)
    col = jnp.tile(jnp.arange(GRID_W), rows)
    inv_freq = ROPE_THETA ** (-jnp.arange(0, A_AXIS_DIM, 2, dtype=jnp.float32) / A_AXIS_DIM)
    ang = jnp.stack([row, col], axis=-1).astype(jnp.float32)[..., None] * inv_freq
    return jnp.cos(ang), jnp.sin(ang)


def apply_axial_rope(x, cos, sin):
    bsz, seq, heads, d = x.shape
    xf = x.astype(jnp.float32).reshape(bsz, seq, heads, 2, A_AXIS_DIM)
    half = A_AXIS_DIM // 2
    x1, x2 = xf[..., :half], xf[..., half:]
    c, s = cos[None, :, None], sin[None, :, None]
    out = jnp.concatenate([x1 * c - x2 * s, x2 * c + x1 * s], axis=-1)
    return out.reshape(bsz, seq, heads, d).astype(x.dtype)


def alibi_slopes(n_heads):
    return jnp.asarray([2.0 ** (-8.0 * (h + 1) / n_heads) for h in range(n_heads)], jnp.float32)


def hgrn_lower_bounds(logits):
    p = jax.nn.softmax(logits.astype(jnp.float32), axis=0)
    return jnp.maximum(jnp.cumsum(p, axis=0) - p[0:1], 0.0)


def gqa_attention(q, k, v):
    bsz, seq, _, d = q.shape
    nb = seq // Q_BLOCK
    qb = q.reshape(bsz, nb, Q_BLOCK, A_KV_HEADS, A_GROUP, d).transpose(1, 0, 2, 3, 4, 5)
    scale = d ** -0.5

    def one(qblk):
        s = jnp.einsum('bqhgd,bkhd->bhgqk', qblk, k).astype(jnp.float32) * scale
        p = jax.nn.softmax(s, axis=-1).astype(v.dtype)
        return jnp.einsum('bhgqk,bkhd->bqhgd', p, v)

    o = lax.map(one, qb)
    return o.transpose(1, 0, 2, 3, 4, 5).reshape(bsz, seq, A_HEADS * d)


def diff_attention(q1, q2, k1, k2, v, lam, slopes):
    bsz, seq, heads, d = q1.shape
    nb = seq // Q_BLOCK
    scale = d ** -0.5
    pos = jnp.arange(seq)
    qpos = pos.reshape(nb, Q_BLOCK)

    def blocks(t):
        return t.reshape(bsz, nb, Q_BLOCK, heads, d).transpose(1, 0, 2, 3, 4)

    def one(args):
        q1b, q2b, qp = args
        dist = jnp.abs(qp[:, None] - pos[None, :]).astype(jnp.float32)
        bias = -slopes[:, None, None] * dist
        s1 = jnp.einsum('bqhd,bkhd->bhqk', q1b, k1).astype(jnp.float32) * scale + bias
        s2 = jnp.einsum('bqhd,bkhd->bhqk', q2b, k2).astype(jnp.float32) * scale + bias
        p = jax.nn.softmax(s1, axis=-1) - lam * jax.nn.softmax(s2, axis=-1)
        return jnp.einsum('bhqk,bkhv->bqhv', p.astype(v.dtype), v)

    o = lax.map(one, (blocks(q1), blocks(q2), qpos))
    return o.transpose(1, 0, 2, 3, 4).reshape(bsz, seq, heads, v.shape[-1])


def gated_linear_scan(q, k, v, log_f):
    q, k, v, log_f = (t.astype(jnp.float32) for t in (q, k, v, log_f))
    bsz, seq, heads, dk = q.shape
    dv = v.shape[-1]
    n = seq // C_CHUNK

    def chunks(t):
        return t.reshape(bsz, n, C_CHUNK, heads, t.shape[-1]).transpose(0, 3, 1, 2, 4)

    q, k, v, log_f = chunks(q), chunks(k), chunks(v), chunks(log_f)
    b = jnp.cumsum(log_f, axis=3)
    b_end = b[:, :, :, -1:, :]
    lower = jnp.tril(jnp.ones((C_CHUNK, C_CHUNK), dtype=bool))[:, :, None]
    diff = b[:, :, :, :, None, :] - b[:, :, :, None, :, :]
    decay = jnp.where(lower, jnp.exp(jnp.where(lower, diff, 0.0)), 0.0)
    intra = jnp.einsum('bhntd,bhnsd,bhntsd->bhnts', q, k, decay)
    o = jnp.einsum('bhnts,bhnsv->bhntv', intra, v)
    kv_chunk = jnp.einsum('bhnsd,bhnsv->nbhdv', k * jnp.exp(b_end - b), v)
    decay_chunk = jnp.exp(b_end[:, :, :, 0, :]).transpose(2, 0, 1, 3)

    def step(state, inp):
        dec, kv = inp
        return dec[..., None] * state + kv, state

    _, states = lax.scan(step, jnp.zeros((bsz, heads, dk, dv), jnp.float32), (decay_chunk, kv_chunk))
    o = o + jnp.einsum('bhntd,nbhdv->bhntv', q * jnp.exp(b), states)
    return o.transpose(0, 2, 3, 1, 4).reshape(bsz, seq, heads, dv)


def depthwise_conv(h, w, b):
    out = lax.conv_general_dilated(h, w[:, None, :].astype(h.dtype), window_strides=(1,), padding='SAME',
                                   dimension_numbers=('NWC', 'WIO', 'NWC'), feature_group_count=h.shape[-1])
    return out + b.astype(h.dtype)


def setup_inputs(seed: int = 0) -> dict:
    key = jax.random.key(seed)
    ks = jax.random.split(key, 17)
    f32 = jnp.float32

    def nrm(k, shape, scale):
        return jax.random.normal(k, shape, f32) * scale

    def gain(k, shape):
        return 1.0 + 0.02 * jax.random.normal(k, shape, f32)

    return {
        'x': nrm(ks[0], (BATCH, SEQ, D_MODEL), 1.0),
        'attn_norm': gain(ks[1], (DEPTH, D_MODEL)),
        'w_in': nrm(ks[2], (DEPTH, D_MODEL, IN_COLS), D_MODEL ** -0.5),
        'a_q_norm': gain(ks[3], (DEPTH, A_HEAD_DIM)),
        'a_k_norm': gain(ks[4], (DEPTH, A_HEAD_DIM)),
        'b_q_norm': gain(ks[5], (DEPTH, B_HEAD_DIM)),
        'b_k_norm': gain(ks[6], (DEPTH, B_HEAD_DIM)),
        'b_lambda': nrm(ks[7], (DEPTH, 4, B_HEAD_DIM), 0.1),
        'b_sub_norm': gain(ks[8], (DEPTH, B_WIDTH)),
        'c_lb_logits': nrm(ks[9], (DEPTH, C_KEY_WIDTH), 0.5),
        'c_out_norm': gain(ks[10], (DEPTH, C_VAL_WIDTH)),
        'w_out': nrm(ks[11], (DEPTH, D_MIX, D_MODEL), D_MIX ** -0.5),
        'ffn_norm': gain(ks[12], (DEPTH, D_MODEL)),
        'w_up': nrm(ks[13], (DEPTH, D_MODEL, 2 * D_FF), D_MODEL ** -0.5),
        'conv_w': nrm(ks[14], (DEPTH, CONV_W, D_FF), CONV_W ** -0.5),
        'conv_b': nrm(ks[15], (DEPTH, D_FF), 0.02),
        'w_down': nrm(ks[16], (DEPTH, D_FF, D_MODEL), D_FF ** -0.5),
    }


def reference(x, attn_norm, w_in, a_q_norm, a_k_norm, b_q_norm, b_k_norm, b_lambda, b_sub_norm,
              c_lb_logits, c_out_norm, w_out, ffn_norm, w_up, conv_w, conv_b, w_down):
    bsz, seq, _ = x.shape
    cos, sin = axial_rope_tables(seq)
    slopes = alibi_slopes(B_HEADS)
    lb_table = hgrn_lower_bounds(c_lb_logits)

    def rev(t):
        return jnp.flip(t, axis=1)

    for l in range(DEPTH):
        h = rms_norm(x, attn_norm[l])
        proj = h @ w_in[l].astype(h.dtype)
        a_q, a_k, a_v, b_q, b_k, b_v, c_q, c_ff, c_fb, c_v, c_g = split_columns(proj)

        aq = apply_axial_rope(rms_norm(a_q.reshape(bsz, seq, A_HEADS, A_HEAD_DIM), a_q_norm[l]), cos, sin)
        ak = apply_axial_rope(rms_norm(a_k.reshape(bsz, seq, A_KV_HEADS, A_HEAD_DIM), a_k_norm[l]), cos, sin)
        av = a_v.reshape(bsz, seq, A_KV_HEADS, A_HEAD_DIM)
        o_a = gqa_attention(aq, ak, av)

        bq = rms_norm(b_q.reshape(bsz, seq, B_HEADS, 2, B_HEAD_DIM), b_q_norm[l])
        bk = rms_norm(b_k.reshape(bsz, seq, B_HEADS, 2, B_HEAD_DIM), b_k_norm[l])
        bv = b_v.reshape(bsz, seq, B_HEADS, B_V_DIM)
        lam_init = 0.8 - 0.6 * math.exp(-0.3 * l)
        lam_p = b_lambda[l].astype(jnp.float32)
        lam = jnp.exp(jnp.sum(lam_p[0] * lam_p[1])) - jnp.exp(jnp.sum(lam_p[2] * lam_p[3])) + lam_init
        o_b = diff_attention(bq[..., 0, :], bq[..., 1, :], bk[..., 0, :], bk[..., 1, :], bv, lam, slopes)
        o_b = (rms_norm(o_b, b_sub_norm[l].reshape(B_HEADS, B_V_DIM)) * (1.0 - lam_init)).reshape(bsz, seq, B_WIDTH)

        lb = lb_table[l]

        def log_forget(z):
            f = lb + (1.0 - lb) * jax.nn.sigmoid(z.astype(jnp.float32))
            return jnp.log(jnp.maximum(f, MIN_FORGET)).reshape(bsz, seq, C_HEADS, C_KEY_DIM)

        cq = jax.nn.silu(c_q).reshape(bsz, seq, C_HEADS, C_KEY_DIM)
        cv = c_v.reshape(bsz, seq, C_HEADS, C_VAL_DIM)
        lf_fwd = log_forget(c_ff)
        lf_bwd = log_forget(c_fb)
        o_fwd = gated_linear_scan(cq, -jnp.expm1(lf_fwd), cv, lf_fwd)
        o_bwd = rev(gated_linear_scan(rev(cq), rev(-jnp.expm1(lf_bwd)), rev(cv), rev(lf_bwd)))
        o_c = rms_norm(o_fwd + o_bwd, c_out_norm[l].reshape(C_HEADS, C_VAL_DIM)).astype(x.dtype)
        o_c = (o_c * jax.nn.silu(c_g).reshape(bsz, seq, C_HEADS, C_VAL_DIM)).reshape(bsz, seq, C_VAL_WIDTH)

        mix = jnp.concatenate([o_a, o_b.astype(x.dtype), o_c], axis=-1)
        x = x + mix @ w_out[l].astype(x.dtype)

        hf = rms_norm(x, ffn_norm[l]) @ w_up[l].astype(x.dtype)
        gate, val = hf[..., :D_FF], hf[..., D_FF:]
        gate = depthwise_conv(gate, conv_w[l], conv_b[l])
        x = x + (jax.nn.gelu(gate, approximate=False) * val) @ w_down[l].astype(x.dtype)
    return x
```

```python
import functools
import math

import jax
import jax.numpy as jnp
import numpy as np
from jax import lax
from jax.experimental import pallas as pl
from jax.experimental.pallas import tpu as pltpu

D_MODEL = 1024
DEPTH = 4
GRID_W = 64
EPS = 1e-6
MIN_FORGET = 1e-20
LOG2E = 1.4426950408889634

A_HEADS, A_KV_HEADS, A_HEAD_DIM = 6, 2, 64
A_GROUP = A_HEADS // A_KV_HEADS
A_AXIS_DIM = A_HEAD_DIM // 2
ROPE_THETA = 10000.0
A_WIDTH = A_HEADS * A_HEAD_DIM
A_KV_WIDTH = A_KV_HEADS * A_HEAD_DIM

B_HEADS, B_HEAD_DIM = 4, 48
B_V_DIM = 2 * B_HEAD_DIM
B_QK_WIDTH = B_HEADS * 2 * B_HEAD_DIM
B_WIDTH = B_HEADS * B_V_DIM

C_HEADS, C_KEY_DIM, C_VAL_DIM = 4, 64, 64
C_WIDTH = C_HEADS * C_KEY_DIM
C_CHUNK = 16

D_MIX = A_WIDTH + B_WIDTH + C_WIDTH
D_FF = 2816

VMEM_LIMIT_BYTES = 56 * 1024 * 1024

_OFF = np.cumsum([0, A_WIDTH, A_KV_WIDTH, A_KV_WIDTH, B_QK_WIDTH, B_QK_WIDTH, B_WIDTH,
                  C_WIDTH, C_WIDTH, C_WIDTH, C_WIDTH, C_WIDTH])

BF16 = jnp.bfloat16
F32 = jnp.float32
LANES = 128


def _dot(a, b):
    return jnp.dot(a, b, preferred_element_type=F32)


def _dot_nt(a, b):
    return lax.dot_general(a, b, (((1,), (1,)), ((), ())), preferred_element_type=F32)


def _sigmoid(z):
    return 1.0 / (1.0 + jnp.exp(-z))


def _proj_kernel(layer, tm,
                 x_ref, an_ref, wa_ref, wb_ref, wc_ref, ga_ref, gb_ref, lbl_ref,
                 cos_ref, sin_ref, ones_a_ref, ones_b_ref,
                 aq_ref, akt_ref, av_ref, bq_ref, bkt_ref, bv_ref,
                 cq_ref, cv_ref, lff_ref, lfb_ref, kf_ref, kb_ref, cg_ref, cvt_ref,
                 scr_ref):
    x = x_ref[...]
    ms = jnp.mean(x * x, axis=-1, keepdims=True)
    h = (x * lax.rsqrt(ms + EPS) * an_ref[...]).astype(BF16)

    cos = cos_ref[...]
    sin = sin_ref[...]

    def norm_rope(y, yp, width, g, gp, scale):
        ones = ones_a_ref[0:width, 0:width]
        ss = _dot((y * y).astype(BF16), ones)
        r = lax.rsqrt(ss * (1.0 / A_HEAD_DIM) + EPS)
        out = (y * g * cos[:, 0:width] + yp * gp * sin[:, 0:width]) * r
        return out * scale if scale != 1.0 else out

    yq = _dot(h, wa_ref[:, 0:2 * A_WIDTH])
    q = norm_rope(yq[:, 0:A_WIDTH], yq[:, A_WIDTH:2 * A_WIDTH], A_WIDTH,
                  ga_ref[0:1, 0:A_WIDTH], ga_ref[1:2, 0:A_WIDTH], A_HEAD_DIM ** -0.5 * LOG2E)
    aq_ref[...] = q.astype(BF16)
    o = 2 * A_WIDTH
    yk = _dot(h, wa_ref[:, o:o + 2 * A_KV_WIDTH])
    k = norm_rope(yk[:, 0:A_KV_WIDTH], yk[:, A_KV_WIDTH:2 * A_KV_WIDTH], A_KV_WIDTH,
                  ga_ref[2:3, 0:A_KV_WIDTH], ga_ref[3:4, 0:A_KV_WIDTH], 1.0)
    akt_ref[...] = k.T.astype(BF16)
    o += 2 * A_KV_WIDTH
    yv = _dot(h, wa_ref[:, o:o + A_KV_WIDTH])
    for j in range(A_KV_HEADS):
        av_ref[j] = yv[:, j * A_HEAD_DIM:(j + 1) * A_HEAD_DIM].astype(BF16)

    def norm_b(y, g, scale):
        ss = _dot((y * y).astype(BF16), ones_b_ref[...])
        r = lax.rsqrt(ss * (1.0 / B_HEAD_DIM) + EPS)
        out = y * g * r
        return out * scale if scale != 1.0 else out

    yq = _dot(h, wb_ref[:, 0:B_QK_WIDTH])
    bq_ref[...] = norm_b(yq, gb_ref[0:1, :], B_HEAD_DIM ** -0.5 * LOG2E).astype(BF16)
    yk = _dot(h, wb_ref[:, B_QK_WIDTH:2 * B_QK_WIDTH])
    bkt_ref[...] = norm_b(yk, gb_ref[1:2, :], 1.0).T.astype(BF16)
    yv = _dot(h, wb_ref[:, 2 * B_QK_WIDTH:2 * B_QK_WIDTH + B_WIDTH])
    for j in range(B_HEADS):
        bv_ref[j] = yv[:, j * B_V_DIM:(j + 1) * B_V_DIM].astype(BF16)

    logits = lbl_ref[...]
    mx = jnp.max(logits, axis=0, keepdims=True)
    e = jnp.exp(logits - mx)
    p = e / jnp.sum(e, axis=0, keepdims=True)
    cum = p[0:1, :]
    for i in range(1, layer + 1):
        cum = cum + p[i:i + 1, :]
    lb = jnp.maximum(cum - p[0:1, :], 0.0)

    nchunk = tm // C_CHUNK

    def interleave(val, out_ref):
        for c in range(C_WIDTH // LANES):
            scr_ref[c] = val[:, c * LANES:(c + 1) * LANES]
        for t in range(C_CHUNK):
            for c in range(C_WIDTH // LANES):
                out_ref[0, t, :, c * LANES:(c + 1) * LANES] = scr_ref[c, pl.ds(t, nchunk, stride=C_CHUNK), :]

    yc = _dot(h, wc_ref[:, 0:C_WIDTH])
    interleave(yc * _sigmoid(yc), cq_ref)
    for idx, (lf_ref, k_ref) in enumerate(((lff_ref, kf_ref), (lfb_ref, kb_ref))):
        z = _dot(h, wc_ref[:, (1 + idx) * C_WIDTH:(2 + idx) * C_WIDTH])
        f = jnp.maximum(lb + (1.0 - lb) * _sigmoid(z), MIN_FORGET)
        interleave(jnp.log(f), lf_ref)
        interleave(1.0 - f, k_ref)
    yv = _dot(h, wc_ref[:, 3 * C_WIDTH:4 * C_WIDTH])
    interleave(yv, cv_ref)
    cvt_ref[...] = yv.T.astype(BF16)
    yg = _dot(h, wc_ref[:, 4 * C_WIDTH:5 * C_WIDTH])
    cg_ref[...] = (yg * _sigmoid(yg)).astype(BF16)


def _proj(layer, xf, an, wa, wb, wc, ga, gb, lbl, cos, sin, ones_a, ones_b, bsz, seq):
    tokens = bsz * seq
    tm = min(512, seq)
    nt = seq // tm
    nchunk = tm // C_CHUNK
    const = lambda shape: pl.BlockSpec(shape, lambda i: (0,) * len(shape))
    tok = lambda w: pl.BlockSpec((tm, w), lambda i: (i, 0))
    tr = lambda w: pl.BlockSpec((w, tm), lambda i: (0, i))
    head = lambda nh, w: pl.BlockSpec((nh, tm, w), lambda i: (0, i, 0))
    inter = pl.BlockSpec((1, C_CHUNK, nchunk, C_WIDTH), lambda i: (i // nt, 0, i % nt, 0))
    inter_shape = jax.ShapeDtypeStruct((bsz, C_CHUNK, seq // C_CHUNK, C_WIDTH), F32)
    out_shape = (
        jax.ShapeDtypeStruct((tokens, A_WIDTH), BF16),
        jax.ShapeDtypeStruct((A_KV_WIDTH, tokens), BF16),
        jax.ShapeDtypeStruct((A_KV_HEADS, tokens, A_HEAD_DIM), BF16),
        jax.ShapeDtypeStruct((tokens, B_QK_WIDTH), BF16),
        jax.ShapeDtypeStruct((B_QK_WIDTH, tokens), BF16),
        jax.ShapeDtypeStruct((B_HEADS, tokens, B_V_DIM), BF16),
        inter_shape, inter_shape, inter_shape, inter_shape, inter_shape, inter_shape,
        jax.ShapeDtypeStruct((tokens, C_WIDTH), BF16),
        jax.ShapeDtypeStruct((C_WIDTH, tokens), BF16),
    )
    out_specs = (
        tok(A_WIDTH), tr(A_KV_WIDTH), head(A_KV_HEADS, A_HEAD_DIM),
        tok(B_QK_WIDTH), tr(B_QK_WIDTH), head(B_HEADS, B_V_DIM),
        inter, inter, inter, inter, inter, inter,
        tok(C_WIDTH), tr(C_WIDTH),
    )
    in_specs = [
        tok(D_MODEL), const((1, D_MODEL)),
        const(wa.shape), const(wb.shape), const(wc.shape),
        const(ga.shape), const(gb.shape), const(lbl.shape),
        pl.BlockSpec((tm, A_WIDTH), lambda i: (i % nt, 0)),
        pl.BlockSpec((tm, A_WIDTH), lambda i: (i % nt, 0)),
        const(ones_a.shape), const(ones_b.shape),
    ]
    return pl.pallas_call(
        functools.partial(_proj_kernel, layer, tm),
        grid=(tokens // tm,),
        in_specs=in_specs,
        out_specs=out_specs,
        out_shape=out_shape,
        scratch_shapes=[pltpu.VMEM((C_WIDTH // LANES, tm, LANES), F32)],
        compiler_params=pltpu.CompilerParams(
            dimension_semantics=("arbitrary",), vmem_limit_bytes=VMEM_LIMIT_BYTES),
        name=f"proj{layer}",
    )(xf, an, wa, wb, wc, ga, gb, lbl, cos, sin, ones_a, ones_b)


def _attn_a_kernel(tq, q_ref, kt_ref, v_ref, o_ref):
    for g in range(A_KV_HEADS):
        heads = [A_GROUP * g + j for j in range(A_GROUP)]
        qg = jnp.concatenate(
            [q_ref[:, h * A_HEAD_DIM:(h + 1) * A_HEAD_DIM] for h in heads], axis=0)
        s = _dot(qg, kt_ref[g * A_HEAD_DIM:(g + 1) * A_HEAD_DIM, :])
        m = jnp.max(s, axis=-1, keepdims=True)
        p = jnp.exp2(s - m)
        l = jnp.sum(p, axis=-1, keepdims=True)
        o = _dot(p.astype(BF16), v_ref[g]) * (1.0 / l)
        for j, h in enumerate(heads):
            o_ref[:, h * A_HEAD_DIM:(h + 1) * A_HEAD_DIM] = o[j * tq:(j + 1) * tq].astype(BF16)


def _attn_a(layer, aq, akt, av, bsz, seq):
    tokens = bsz * seq
    tq = min(128, seq)
    nq = seq // tq
    return pl.pallas_call(
        functools.partial(_attn_a_kernel, tq),
        grid=(bsz, nq),
        in_specs=[
            pl.BlockSpec((tq, A_WIDTH), lambda b, i: (b * nq + i, 0)),
            pl.BlockSpec((A_KV_WIDTH, seq), lambda b, i: (0, b)),
            pl.BlockSpec((A_KV_HEADS, seq, A_HEAD_DIM), lambda b, i: (0, b, 0)),
        ],
        out_specs=pl.BlockSpec((tq, A_WIDTH), lambda b, i: (b * nq + i, 0)),
        out_shape=jax.ShapeDtypeStruct((tokens, A_WIDTH), BF16),
        compiler_params=pltpu.CompilerParams(
            dimension_semantics=("arbitrary", "arbitrary"), vmem_limit_bytes=VMEM_LIMIT_BYTES),
        name=f"attn_a{layer}",
    )(aq, akt, av)


def _attn_b_kernel(layer, tq, nq, seq, q_ref, kt_ref, v_ref, dist_ref, lam_ref, gain_ref, o_ref):
    i = pl.program_id(1)
    lam_p = lam_ref[...]
    lam_init = 0.8 - 0.6 * math.exp(-0.3 * layer)
    lam = (jnp.exp(jnp.sum(lam_p[0:1] * lam_p[1:2], axis=-1, keepdims=True))
           - jnp.exp(jnp.sum(lam_p[2:3] * lam_p[3:4], axis=-1, keepdims=True)) + lam_init)
    off = pl.multiple_of((nq - 1 - i) * tq, tq)
    dist = dist_ref[:, pl.ds(off, seq)]
    for h in range(B_HEADS):
        slope = 2.0 ** (-8.0 * (h + 1) / B_HEADS)
        bias = dist * slope
        outs = []
        for c in range(2):
            lo = (2 * h + c) * B_HEAD_DIM
            s = _dot(q_ref[:, lo:lo + B_HEAD_DIM], kt_ref[lo:lo + B_HEAD_DIM, :]) + bias
            m = jnp.max(s, axis=-1, keepdims=True)
            p = jnp.exp2(s - m)
            l = jnp.sum(p, axis=-1, keepdims=True)
            outs.append(_dot(p.astype(BF16), v_ref[h]) * (1.0 / l))
        o = outs[0] - lam * outs[1]
        ms = jnp.mean(o * o, axis=-1, keepdims=True)
        o = o * lax.rsqrt(ms + EPS) * gain_ref[:, h * B_V_DIM:(h + 1) * B_V_DIM] * (1.0 - lam_init)
        o_ref[:, h * B_V_DIM:(h + 1) * B_V_DIM] = o.astype(BF16)


def _attn_b(layer, bq, bkt, bv, dist, lam, gain, bsz, seq):
    tokens = bsz * seq
    tq = min(128, seq)
    nq = seq // tq
    return pl.pallas_call(
        functools.partial(_attn_b_kernel, layer, tq, nq, seq),
        grid=(bsz, nq),
        in_specs=[
            pl.BlockSpec((tq, B_QK_WIDTH), lambda b, i: (b * nq + i, 0)),
            pl.BlockSpec((B_QK_WIDTH, seq), lambda b, i: (0, b)),
            pl.BlockSpec((B_HEADS, seq, B_V_DIM), lambda b, i: (0, b, 0)),
            pl.BlockSpec(dist.shape, lambda b, i: (0, 0)),
            pl.BlockSpec(lam.shape, lambda b, i: (0, 0)),
            pl.BlockSpec(gain.shape, lambda b, i: (0, 0)),
        ],
        out_specs=pl.BlockSpec((tq, B_WIDTH), lambda b, i: (b * nq + i, 0)),
        out_shape=jax.ShapeDtypeStruct((tokens, B_WIDTH), BF16),
        compiler_params=pltpu.CompilerParams(
            dimension_semantics=("arbitrary", "arbitrary"), vmem_limit_bytes=VMEM_LIMIT_BYTES),
        name=f"attn_b{layer}",
    )(bq, bkt, bv, dist, lam, gain)


_C_BLOCK = 16
_C_TILE = 128
_C_NPAIR = C_CHUNK * (C_CHUNK + 1) // 2


def _hgrn_kernel(seq, q_ref, v_ref, lff_ref, lfb_ref, kf_ref, kb_ref, vt_ref, g_ref, r_ref, gain_ref,
                 o_ref,
                 b_scr, dec_scr, qn_scr, kn_scr, tt_scr, oi_scr, st_scr, ot_scr, on_scr):
    nchunk = seq // C_CHUNK
    ntile = seq // _C_TILE
    cpt = _C_TILE // C_CHUNK

    rows_h = lax.broadcasted_iota(jnp.int32, (C_WIDTH, C_WIDTH), 0) // C_VAL_DIM
    cols_h = lax.broadcasted_iota(jnp.int32, (C_WIDTH, C_WIDTH), 1) // C_KEY_DIM
    same_head = rows_h == cols_h
    lane_chunk = lax.broadcasted_iota(jnp.int32, (C_WIDTH, _C_TILE), 1) // C_CHUNK
    row_chunk = lax.broadcasted_iota(jnp.int32, (_C_TILE, C_WIDTH), 0) // C_CHUNK

    nhalf = C_WIDTH // LANES

    def scatter(dst_ref, t, val):
        for c in range(nhalf):
            dst_ref[c, pl.ds(t, nchunk, stride=C_CHUNK), :] = val[:, c * LANES:(c + 1) * LANES]

    def token_rows(src_ref, c0):
        return jnp.concatenate([src_ref[c, pl.ds(c0, _C_TILE), :] for c in range(nhalf)], axis=1)

    for d, (lf_ref, k_ref) in enumerate(((lff_ref, kf_ref), (lfb_ref, kb_ref))):
        pos = (lambda j: j) if d == 0 else (lambda j: C_CHUNK - 1 - j)

        b = None
        for j in range(C_CHUNK):
            lfj = lf_ref[0, pos(j)]
            b = lfj if b is None else b + lfj
            b_scr[j] = b
        bend = b
        dec_scr[...] = jnp.exp(bend)
        for j in range(C_CHUNK):
            bj = b_scr[j]
            scatter(qn_scr, pos(j), q_ref[0, pos(j)] * jnp.exp(bj))
            scatter(kn_scr, pos(j), k_ref[0, pos(j)] * jnp.exp(bend - bj))

        def intra(cb, carry):
            rows = pl.ds(pl.multiple_of(cb * _C_BLOCK, _C_BLOCK), _C_BLOCK)
            pi = 0
            for j in range(C_CHUNK):
                bj = b_scr[j, rows, :]
                qj = q_ref[0, pos(j), rows, :]
                for i in range(j + 1):
                    w = jnp.exp(bj - b_scr[i, rows, :]) * qj * k_ref[0, pos(i), rows, :]
                    tt_scr[pi * _C_BLOCK:(pi + 1) * _C_BLOCK, :] = w.astype(BF16)
                    pi += 1
            y = _dot(tt_scr[...], r_ref[...])
            pi = 0
            for j in range(C_CHUNK):
                acc = None
                for i in range(j + 1):
                    term = y[pi * _C_BLOCK:(pi + 1) * _C_BLOCK, :] * v_ref[0, pos(i), rows, :]
                    acc = term if acc is None else acc + term
                    pi += 1
                if d == 0:
                    oi_scr[pos(j), rows, :] = acc
                else:
                    oi_scr[pos(j), rows, :] = oi_scr[pos(j), rows, :] + acc
            return carry

        lax.fori_loop(0, nchunk // _C_BLOCK, intra, 0)

        st_scr[...] = jnp.zeros_like(st_scr)

        def inter(ti, carry):
            tile = ti if d == 0 else ntile - 1 - ti
            c0 = pl.multiple_of(tile * _C_TILE, _C_TILE)
            kt = token_rows(kn_scr, c0).astype(BF16)
            qt = token_rows(qn_scr, c0).astype(BF16)
            vt = vt_ref[:, pl.ds(c0, _C_TILE)]
            vexp = jnp.concatenate(
                [jnp.where(lane_chunk == n, vt, jnp.zeros_like(vt)) for n in range(cpt)], axis=0)
            kv = _dot(vexp, kt)
            acc = jnp.zeros((C_WIDTH, _C_TILE), F32)
            for nn in range(cpt):
                n = nn if d == 0 else cpt - 1 - nn
                st = st_scr[...]
                qm = jnp.where(row_chunk == n, qt, jnp.zeros_like(qt))
                acc = acc + _dot_nt(st.astype(BF16), qm)
                drow = dec_scr[pl.ds(tile * cpt + n, 1), :]
                kvn = kv[n * C_WIDTH:(n + 1) * C_WIDTH, :]
                st_scr[...] = st * drow + jnp.where(same_head, kvn, 0.0)
            if d == 0:
                ot_scr[:, pl.ds(c0, _C_TILE)] = acc
            else:
                ot_scr[:, pl.ds(c0, _C_TILE)] = ot_scr[:, pl.ds(c0, _C_TILE)] + acc
            return carry

        lax.fori_loop(0, ntile, inter, 0)

    for t in range(C_CHUNK):
        scatter(on_scr, t, oi_scr[t])

    def finish(ti, carry):
        c0 = pl.multiple_of(ti * _C_TILE, _C_TILE)
        o = token_rows(on_scr, c0) + ot_scr[:, pl.ds(c0, _C_TILE)].T
        ss = _dot((o * o).astype(BF16), r_ref[...])
        o = o * lax.rsqrt(ss * (1.0 / C_VAL_DIM) + EPS) * gain_ref[...]
        o_ref[pl.ds(c0, _C_TILE), :] = (o * g_ref[pl.ds(c0, _C_TILE), :].astype(F32)).astype(BF16)
        return carry

    lax.fori_loop(0, ntile, finish, 0)


def _hgrn(layer, cq, cv, lff, lfb, kf, kb, cvt, cg, ones_c, gain, bsz, seq):
    tokens = bsz * seq
    nchunk = seq // C_CHUNK
    inter = pl.BlockSpec((1, C_CHUNK, nchunk, C_WIDTH), lambda b: (b, 0, 0, 0))
    return pl.pallas_call(
        functools.partial(_hgrn_kernel, seq),
        grid=(bsz,),
        in_specs=[
            inter, inter, inter, inter, inter, inter,
            pl.BlockSpec((C_WIDTH, seq), lambda b: (0, b)),
            pl.BlockSpec((seq, C_WIDTH), lambda b: (b, 0)),
            pl.BlockSpec((C_WIDTH, C_WIDTH), lambda b: (0, 0)),
            pl.BlockSpec((1, C_WIDTH), lambda b: (0, 0)),
        ],
        out_specs=pl.BlockSpec((seq, C_WIDTH), lambda b: (b, 0)),
        out_shape=jax.ShapeDtypeStruct((tokens, C_WIDTH), BF16),
        scratch_shapes=[
            pltpu.VMEM((C_CHUNK, nchunk, C_WIDTH), F32),
            pltpu.VMEM((nchunk, C_WIDTH), F32),
            pltpu.VMEM((C_WIDTH // LANES, seq, LANES), F32),
            pltpu.VMEM((C_WIDTH // LANES, seq, LANES), F32),
            pltpu.VMEM((_C_NPAIR * _C_BLOCK, C_WIDTH), BF16),
            pltpu.VMEM((C_CHUNK, nchunk, C_WIDTH), F32),
            pltpu.VMEM((C_WIDTH, C_WIDTH), F32),
            pltpu.VMEM((C_WIDTH, seq), F32),
            pltpu.VMEM((C_WIDTH // LANES, seq, LANES), F32),
        ],
        compiler_params=pltpu.CompilerParams(
            dimension_semantics=("arbitrary",), vmem_limit_bytes=VMEM_LIMIT_BYTES),
        name=f"hgrn{layer}",
    )(cq, cv, lff, lfb, kf, kb, cvt, cg, ones_c, gain)


_F_BLOCK = 256


def _gelu(x):
    return 0.5 * x * (1.0 + lax.erf(x * (2.0 ** -0.5)))


def _ffn_kernel(seq, x_ref, oa_ref, ob_ref, oc_ref, wo_ref, fn_ref, wg_ref, wv_ref, cw_ref, cb_ref, wd_ref,
                o_ref, h_scr):
    j = pl.program_id(1)

    @pl.when(j == 0)
    def _():
        mix = jnp.concatenate([oa_ref[...], ob_ref[...], oc_ref[...]], axis=1)
        xm = x_ref[...] + _dot(mix, wo_ref[...])
        o_ref[...] = xm
        ms = jnp.mean(xm * xm, axis=-1, keepdims=True)
        h_scr[...] = (xm * lax.rsqrt(ms + EPS) * fn_ref[...]).astype(BF16)

    h = h_scr[...]
    gate = _dot(h, wg_ref[...])
    val = _dot(h, wv_ref[...])
    row = lax.broadcasted_iota(jnp.int32, gate.shape, 0)
    prev = jnp.where(row == 0, 0.0, pltpu.roll(gate, 1, 0))
    nxt = jnp.where(row == seq - 1, 0.0, pltpu.roll(gate, seq - 1, 0))
    cw = cw_ref[...]
    conv = prev * cw[0:1, :] + gate * cw[1:2, :] + nxt * cw[2:3, :] + cb_ref[...]
    act = (_gelu(conv) * val).astype(BF16)
    o_ref[...] += _dot(act, wd_ref[...])


def _ffn(layer, xf, oa, ob, oc, wo, fn, wup, cw, cb, wd, bsz, seq):
    tokens = bsz * seq
    nf = D_FF // _F_BLOCK
    single = pl.Buffered(1)
    return pl.pallas_call(
        functools.partial(_ffn_kernel, seq),
        grid=(bsz, nf),
        in_specs=[
            pl.BlockSpec((seq, D_MODEL), lambda b, j: (b, 0), pipeline_mode=single),
            pl.BlockSpec((seq, A_WIDTH), lambda b, j: (b, 0), pipeline_mode=single),
            pl.BlockSpec((seq, B_WIDTH), lambda b, j: (b, 0), pipeline_mode=single),
            pl.BlockSpec((seq, C_WIDTH), lambda b, j: (b, 0), pipeline_mode=single),
            pl.BlockSpec((D_MIX, D_MODEL), lambda b, j: (0, 0), pipeline_mode=single),
            pl.BlockSpec((1, D_MODEL), lambda b, j: (0, 0)),
            pl.BlockSpec((D_MODEL, _F_BLOCK), lambda b, j: (0, j)),
            pl.BlockSpec((D_MODEL, _F_BLOCK), lambda b, j: (0, j + nf)),
            pl.BlockSpec((3, _F_BLOCK), lambda b, j: (0, j)),
            pl.BlockSpec((1, _F_BLOCK), lambda b, j: (0, j)),
            pl.BlockSpec((_F_BLOCK, D_MODEL), lambda b, j: (j, 0)),
        ],
        out_specs=pl.BlockSpec((seq, D_MODEL), lambda b, j: (b, 0)),
        out_shape=jax.ShapeDtypeStruct((tokens, D_MODEL), F32),
        scratch_shapes=[pltpu.VMEM((seq, D_MODEL), BF16)],
        compiler_params=pltpu.CompilerParams(
            dimension_semantics=("arbitrary", "arbitrary"), vmem_limit_bytes=VMEM_LIMIT_BYTES),
        name=f"ffn{layer}",
    )(xf, oa, ob, oc, wo, fn, wup, wup, cw, cb, wd)


def _rope_tables(seq):
    rows = seq // GRID_W
    row = jnp.repeat(jnp.arange(rows), GRID_W)
    col = jnp.tile(jnp.arange(GRID_W), rows)
    inv_freq = ROPE_THETA ** (-jnp.arange(0, A_AXIS_DIM, 2, dtype=F32) / A_AXIS_DIM)
    ang = jnp.stack([row, col], axis=-1).astype(F32)[..., None] * inv_freq
    cos, sin = jnp.cos(ang), jnp.sin(ang)
    cos_h = jnp.concatenate([cos[:, 0], cos[:, 0], cos[:, 1], cos[:, 1]], axis=-1)
    sin_h = jnp.concatenate([-sin[:, 0], sin[:, 0], -sin[:, 1], sin[:, 1]], axis=-1)
    return jnp.tile(cos_h, (1, A_HEADS)), jnp.tile(sin_h, (1, A_HEADS))


def _partner_index(width):
    half = A_AXIS_DIM // 2
    i = np.arange(width)
    return np.where((i % A_AXIS_DIM) < half, i + half, i - half)


def _block_ones(width, block):
    i = np.arange(width) // block
    return jnp.asarray(i[:, None] == i[None, :], BF16)


def kernel(x, attn_norm, w_in, a_q_norm, a_k_norm, b_q_norm, b_k_norm, b_lambda, b_sub_norm,
           c_lb_logits, c_out_norm, w_out, ffn_norm, w_up, conv_w, conv_b, w_down):
    bsz, seq, _ = x.shape
    tokens = bsz * seq
    assert seq % 256 == 0 and seq % GRID_W == 0

    cos, sin = _rope_tables(seq)
    ones_a = _block_ones(A_WIDTH, A_HEAD_DIM)
    ones_b = _block_ones(B_QK_WIDTH, B_HEAD_DIM)
    ones_c = _block_ones(C_WIDTH, C_VAL_DIM)
    pq = _partner_index(A_WIDTH)
    pk = _partner_index(A_KV_WIDTH)

    tq = min(128, seq)
    r = jnp.arange(tq, dtype=jnp.int32)[:, None]
    u = jnp.arange(2 * seq - tq, dtype=jnp.int32)[None, :]
    dist = (-LOG2E) * jnp.abs(r - u + (seq - tq)).astype(F32)

    xf = x.reshape(tokens, D_MODEL)
    for l in range(DEPTH):
        w = w_in[l]
        wq, wk, wv = (w[:, _OFF[0]:_OFF[1]], w[:, _OFF[1]:_OFF[2]], w[:, _OFF[2]:_OFF[3]])
        wa = jnp.concatenate([wq, wq[:, pq], wk, wk[:, pk], wv], axis=1).astype(BF16)
        wb = w[:, _OFF[3]:_OFF[6]].astype(BF16)
        wc = w[:, _OFF[6]:_OFF[11]].astype(BF16)
        gq = jnp.tile(a_q_norm[l], A_HEADS)
        gk = jnp.pad(jnp.tile(a_k_norm[l], A_KV_HEADS), (0, A_WIDTH - A_KV_WIDTH))
        gkp = jnp.pad(jnp.tile(a_k_norm[l], A_KV_HEADS)[pk], (0, A_WIDTH - A_KV_WIDTH))
        ga = jnp.stack([gq, gq[pq], gk, gkp])
        gb = jnp.stack([jnp.tile(b_q_norm[l], 2 * B_HEADS), jnp.tile(b_k_norm[l], 2 * B_HEADS)])

        (aq, akt, av, bq, bkt, bv, cq, cv, lff, lfb, kf, kb, cg, cvt) = _proj(
            l, xf, attn_norm[l][None, :], wa, wb, wc, ga, gb, c_lb_logits, cos, sin, ones_a, ones_b,
            bsz, seq)
        oa = _attn_a(l, aq, akt, av, bsz, seq)
        ob = _attn_b(l, bq, bkt, bv, dist, b_lambda[l], b_sub_norm[l][None, :], bsz, seq)
        oc = _hgrn(l, cq, cv, lff, lfb, kf, kb, cvt, cg, ones_c, c_out_norm[l][None, :], bsz, seq)
        xf = _ffn(l, xf, oa, ob, oc, w_out[l].astype(BF16), ffn_norm[l][None, :],
                  w_up[l].astype(BF16), conv_w[l], conv_b[l][None, :], w_down[l].astype(BF16), bsz, seq)
    return xf.reshape(bsz, seq, D_MODEL)
```

```python
import functools
import math

import jax
import jax.numpy as jnp
import numpy as np
from jax import lax
from jax.experimental import pallas as pl
from jax.experimental.pallas import tpu as pltpu

D_MODEL = 1024
DEPTH = 4
GRID_W = 64
EPS = 1e-6
MIN_FORGET = 1e-20
LOG2E = 1.4426950408889634

A_HEADS, A_KV_HEADS, A_HEAD_DIM = 6, 2, 64
A_GROUP = A_HEADS // A_KV_HEADS
A_AXIS_DIM = A_HEAD_DIM // 2
ROPE_THETA = 10000.0
A_WIDTH = A_HEADS * A_HEAD_DIM
A_KV_WIDTH = A_KV_HEADS * A_HEAD_DIM

B_HEADS, B_HEAD_DIM = 4, 48
B_V_DIM = 2 * B_HEAD_DIM
B_QK_WIDTH = B_HEADS * 2 * B_HEAD_DIM
B_WIDTH = B_HEADS * B_V_DIM

C_HEADS, C_KEY_DIM, C_VAL_DIM = 4, 64, 64
C_WIDTH = C_HEADS * C_KEY_DIM
C_CHUNK = 16

D_MIX = A_WIDTH + B_WIDTH + C_WIDTH
D_FF = 2816

VMEM_LIMIT_BYTES = 56 * 1024 * 1024

_OFF = np.cumsum([0, A_WIDTH, A_KV_WIDTH, A_KV_WIDTH, B_QK_WIDTH, B_QK_WIDTH, B_WIDTH,
                  C_WIDTH, C_WIDTH, C_WIDTH, C_WIDTH, C_WIDTH])

BF16 = jnp.bfloat16
F32 = jnp.float32
LANES = 128


def _dot(a, b):
    return jnp.dot(a, b, preferred_element_type=F32)


def _dot_nt(a, b):
    return lax.dot_general(a, b, (((1,), (1,)), ((), ())), preferred_element_type=F32)


def _sigmoid(z):
    return 1.0 / (1.0 + jnp.exp(-z))


def _proj_kernel(layer, tm,
                 x_ref, an_ref, wa_ref, wb_ref, wc_ref, ga_ref, gb_ref, lbl_ref,
                 cos_ref, sin_ref, ones_a_ref, ones_b_ref,
                 aq_ref, akt_ref, av_ref, bq_ref, bkt_ref, bv_ref,
                 cq_ref, cv_ref, lff_ref, lfb_ref, kf_ref, kb_ref, cg_ref, cvt_ref,
                 scr_ref):
    x = x_ref[...]
    ms = jnp.mean(x * x, axis=-1, keepdims=True)
    h = (x * lax.rsqrt(ms + EPS) * an_ref[...]).astype(BF16)

    cos = cos_ref[...]
    sin = sin_ref[...]
    nchunk = tm // C_CHUNK

    def norm_rope(y, yp, width, g, gp, scale):
        ones = ones_a_ref[0:width, 0:width]
        ss = _dot((y * y).astype(BF16), ones)
        r = lax.rsqrt(ss * (1.0 / A_HEAD_DIM) + EPS)
        out = (y * g * cos[:, 0:width] + yp * gp * sin[:, 0:width]) * r
        return out * scale if scale != 1.0 else out

    def norm_b(y, g, scale):
        ss = _dot((y * y).astype(BF16), ones_b_ref[...])
        r = lax.rsqrt(ss * (1.0 / B_HEAD_DIM) + EPS)
        out = y * g * r
        return out * scale if scale != 1.0 else out

    def with_ones_column(v):
        tail = (lax.broadcasted_iota(jnp.int32, (tm, LANES - v.shape[1]), 1) == 0).astype(F32)
        return jnp.concatenate([v, tail], axis=1).astype(BF16)

    def interleave(slot, val, out_ref):
        for c in range(C_WIDTH // LANES):
            scr_ref[slot, c] = val[:, c * LANES:(c + 1) * LANES]
        for t in range(C_CHUNK):
            for c in range(C_WIDTH // LANES):
                out_ref[0, t, :, c * LANES:(c + 1) * LANES] = scr_ref[slot, c, pl.ds(t, nchunk, stride=C_CHUNK), :]

    def forget_lower_bound():
        logits = lbl_ref[...]
        mx = jnp.max(logits, axis=0, keepdims=True)
        e = jnp.exp(logits - mx)
        p = e / jnp.sum(e, axis=0, keepdims=True)
        cum = p[0:1, :]
        for i in range(1, layer + 1):
            cum = cum + p[i:i + 1, :]
        return jnp.maximum(cum - p[0:1, :], 0.0)

    def epi_aq(y):
        q = norm_rope(y[:, 0:A_WIDTH], y[:, A_WIDTH:2 * A_WIDTH], A_WIDTH,
                      ga_ref[0:1, 0:A_WIDTH], ga_ref[1:2, 0:A_WIDTH], A_HEAD_DIM ** -0.5 * LOG2E)
        aq_ref[...] = q.astype(BF16)

    def epi_ak(y):
        k = norm_rope(y[:, 0:A_KV_WIDTH], y[:, A_KV_WIDTH:2 * A_KV_WIDTH], A_KV_WIDTH,
                      ga_ref[2:3, 0:A_KV_WIDTH], ga_ref[3:4, 0:A_KV_WIDTH], 1.0)
        akt_ref[...] = k.T.astype(BF16)

    def epi_av(y):
        for j in range(A_KV_HEADS):
            av_ref[j] = with_ones_column(y[:, j * A_HEAD_DIM:(j + 1) * A_HEAD_DIM])

    def epi_bq(y):
        bq_ref[...] = norm_b(y, gb_ref[0:1, :], B_HEAD_DIM ** -0.5 * LOG2E).astype(BF16)

    def epi_bk(y):
        bkt_ref[...] = norm_b(y, gb_ref[1:2, :], 1.0).T.astype(BF16)

    def epi_bv(y):
        for j in range(B_HEADS):
            bv_ref[j] = with_ones_column(y[:, j * B_V_DIM:(j + 1) * B_V_DIM])

    def epi_cq(y):
        interleave(0, y * _sigmoid(y), cq_ref)

    def epi_forget(slot, lf_ref, k_ref):
        def epi(z):
            lb = forget_lower_bound()
            f = jnp.maximum(lb + (1.0 - lb) * _sigmoid(z), MIN_FORGET)
            interleave(slot, jnp.log(f), lf_ref)
            interleave(slot + 1, 1.0 - f, k_ref)
        return epi

    def epi_cv(y):
        interleave(5, y, cv_ref)
        cvt_ref[...] = y.T.astype(BF16)

    def epi_cg(y):
        cg_ref[...] = (y * _sigmoid(y)).astype(BF16)

    oa_k = 2 * A_WIDTH
    oa_v = oa_k + 2 * A_KV_WIDTH
    groups = [
        (wa_ref, 0, 2 * A_WIDTH, epi_aq),
        (wa_ref, oa_k, 2 * A_KV_WIDTH, epi_ak),
        (wa_ref, oa_v, A_KV_WIDTH, epi_av),
        (wb_ref, 0, B_QK_WIDTH, epi_bq),
        (wb_ref, B_QK_WIDTH, B_QK_WIDTH, epi_bk),
        (wb_ref, 2 * B_QK_WIDTH, B_WIDTH, epi_bv),
        (wc_ref, 0, C_WIDTH, epi_cq),
        (wc_ref, C_WIDTH, C_WIDTH, epi_forget(1, lff_ref, kf_ref)),
        (wc_ref, 2 * C_WIDTH, C_WIDTH, epi_forget(3, lfb_ref, kb_ref)),
        (wc_ref, 3 * C_WIDTH, C_WIDTH, epi_cv),
        (wc_ref, 4 * C_WIDTH, C_WIDTH, epi_cg),
    ]
    pending = None
    for w_ref, lo, width, epi in groups:
        y = _dot(h, w_ref[:, lo:lo + width])
        if pending is not None:
            pending[0](pending[1])
        pending = (epi, y)
    pending[0](pending[1])


def _proj(layer, xf, an, wa, wb, wc, ga, gb, lbl, cos, sin, ones_a, ones_b, bsz, seq):
    tokens = bsz * seq
    tm = min(512, seq)
    nt = seq // tm
    nchunk = tm // C_CHUNK
    const = lambda shape: pl.BlockSpec(shape, lambda i: (0,) * len(shape))
    tok = lambda w: pl.BlockSpec((tm, w), lambda i: (i, 0))
    tr = lambda w: pl.BlockSpec((w, tm), lambda i: (0, i))
    head = lambda nh, w: pl.BlockSpec((nh, tm, w), lambda i: (0, i, 0))
    inter = pl.BlockSpec((1, C_CHUNK, nchunk, C_WIDTH), lambda i: (i // nt, 0, i % nt, 0))
    inter_shape = jax.ShapeDtypeStruct((bsz, C_CHUNK, seq // C_CHUNK, C_WIDTH), F32)
    out_shape = (
        jax.ShapeDtypeStruct((tokens, A_WIDTH), BF16),
        jax.ShapeDtypeStruct((A_KV_WIDTH, tokens), BF16),
        jax.ShapeDtypeStruct((A_KV_HEADS, tokens, LANES), BF16),
        jax.ShapeDtypeStruct((tokens, B_QK_WIDTH), BF16),
        jax.ShapeDtypeStruct((B_QK_WIDTH, tokens), BF16),
        jax.ShapeDtypeStruct((B_HEADS, tokens, LANES), BF16),
        inter_shape, inter_shape, inter_shape, inter_shape, inter_shape, inter_shape,
        jax.ShapeDtypeStruct((tokens, C_WIDTH), BF16),
        jax.ShapeDtypeStruct((C_WIDTH, tokens), BF16),
    )
    out_specs = (
        tok(A_WIDTH), tr(A_KV_WIDTH), head(A_KV_HEADS, LANES),
        tok(B_QK_WIDTH), tr(B_QK_WIDTH), head(B_HEADS, LANES),
        inter, inter, inter, inter, inter, inter,
        tok(C_WIDTH), tr(C_WIDTH),
    )
    in_specs = [
        tok(D_MODEL), const((1, D_MODEL)),
        const(wa.shape), const(wb.shape), const(wc.shape),
        const(ga.shape), const(gb.shape), const(lbl.shape),
        pl.BlockSpec((tm, A_WIDTH), lambda i: (i % nt, 0)),
        pl.BlockSpec((tm, A_WIDTH), lambda i: (i % nt, 0)),
        const(ones_a.shape), const(ones_b.shape),
    ]
    return pl.pallas_call(
        functools.partial(_proj_kernel, layer, tm),
        grid=(tokens // tm,),
        in_specs=in_specs,
        out_specs=out_specs,
        out_shape=out_shape,
        scratch_shapes=[pltpu.VMEM((6, C_WIDTH // LANES, tm, LANES), F32)],
        compiler_params=pltpu.CompilerParams(
            dimension_semantics=("arbitrary",), vmem_limit_bytes=VMEM_LIMIT_BYTES),
        name=f"proj{layer}",
    )(xf, an, wa, wb, wc, ga, gb, lbl, cos, sin, ones_a, ones_b)


_L_MIN = 2.0 ** -100
_BOUND_SLACK = 1.02


def _score_bound(g_ref, dim):
    g = jnp.abs(g_ref[...])
    gq = jnp.max(g[0:1, :], axis=-1, keepdims=True)
    gk = jnp.max(g[1:2, :], axis=-1, keepdims=True)
    return gq * gk * (dim ** 0.5 * LOG2E * _BOUND_SLACK)


def _attn_a_kernel(tq, tk, q_ref, kt_ref, v_ref, g_ref, o_ref):
    seq = kt_ref.shape[1]
    nk = seq // tk
    bound = _score_bound(g_ref, A_HEAD_DIM)

    def group_q(g):
        heads = [A_GROUP * g + j for j in range(A_GROUP)]
        return heads, jnp.concatenate(
            [q_ref[:, h * A_HEAD_DIM:(h + 1) * A_HEAD_DIM] for h in heads], axis=0)

    def scores(qg, g, c):
        return _dot(qg, kt_ref[g * A_HEAD_DIM:(g + 1) * A_HEAD_DIM, c * tk:(c + 1) * tk])

    def store(g, heads, o):
        for j, h in enumerate(heads):
            o_ref[:, h * A_HEAD_DIM:(h + 1) * A_HEAD_DIM] = o[j * tq:(j + 1) * tq, 0:A_HEAD_DIM].astype(BF16)

    lmin = None
    for g in range(A_KV_HEADS):
        heads, qg = group_q(g)
        acc = None
        for c in range(nk):
            p = jnp.exp2(scores(qg, g, c) - bound).astype(BF16)
            pv = _dot(p, v_ref[g, c * tk:(c + 1) * tk, :])
            acc = pv if acc is None else acc + pv
        l = acc[:, A_HEAD_DIM:A_HEAD_DIM + 1]
        store(g, heads, acc * (1.0 / l))
        lg = jnp.min(l, axis=0, keepdims=True)
        lmin = lg if lmin is None else jnp.minimum(lmin, lg)

    @pl.when(jnp.logical_not(lmin[0, 0] >= _L_MIN))
    def _():
        for g in range(A_KV_HEADS):
            heads, qg = group_q(g)
            m = acc = None
            for c in range(nk):
                s = scores(qg, g, c)
                mc = jnp.max(s, axis=-1, keepdims=True)
                vc = v_ref[g, c * tk:(c + 1) * tk, :]
                if c == 0:
                    m = mc
                    acc = _dot(jnp.exp2(s - m).astype(BF16), vc)
                else:
                    mn = jnp.maximum(m, mc)
                    acc = jnp.exp2(m - mn) * acc + _dot(jnp.exp2(s - mn).astype(BF16), vc)
                    m = mn
            store(g, heads, acc * (1.0 / acc[:, A_HEAD_DIM:A_HEAD_DIM + 1]))


def _attn_a(layer, aq, akt, av, gains, bsz, seq):
    tokens = bsz * seq
    tq = min(512, seq)
    tk = min(256, seq)
    nq = seq // tq
    return pl.pallas_call(
        functools.partial(_attn_a_kernel, tq, tk),
        grid=(bsz, nq),
        in_specs=[
            pl.BlockSpec((tq, A_WIDTH), lambda b, i: (b * nq + i, 0)),
            pl.BlockSpec((A_KV_WIDTH, seq), lambda b, i: (0, b)),
            pl.BlockSpec((A_KV_HEADS, seq, LANES), lambda b, i: (0, b, 0)),
            pl.BlockSpec(gains.shape, lambda b, i: (0, 0)),
        ],
        out_specs=pl.BlockSpec((tq, A_WIDTH), lambda b, i: (b * nq + i, 0)),
        out_shape=jax.ShapeDtypeStruct((tokens, A_WIDTH), BF16),
        compiler_params=pltpu.CompilerParams(
            dimension_semantics=("arbitrary", "arbitrary"), vmem_limit_bytes=VMEM_LIMIT_BYTES),
        name=f"attn_a{layer}",
    )(aq, akt, av, gains)


def _attn_b_kernel(layer, tq, tk, nq, seq, q_ref, kt_ref, v_ref, dist_ref, lam_ref, gain_ref, g_ref, o_ref):
    i = pl.program_id(1)
    lam_p = lam_ref[...]
    lam_init = 0.8 - 0.6 * math.exp(-0.3 * layer)
    lam = (jnp.exp(jnp.sum(lam_p[0:1] * lam_p[1:2], axis=-1, keepdims=True))
           - jnp.exp(jnp.sum(lam_p[2:3] * lam_p[3:4], axis=-1, keepdims=True)) + lam_init)
    off = pl.multiple_of((nq - 1 - i) * tq, tq)
    nk = seq // tk
    bound = _score_bound(g_ref, B_HEAD_DIM)

    def scores(h, c, kc):
        lo = (2 * h + c) * B_HEAD_DIM
        slope = 2.0 ** (-8.0 * (h + 1) / B_HEADS)
        bias = dist_ref[:, pl.ds(off + kc * tk, tk)] * slope
        return _dot(q_ref[:, lo:lo + B_HEAD_DIM], kt_ref[lo:lo + B_HEAD_DIM, kc * tk:(kc + 1) * tk]) + bias

    def finish(h, outs):
        o = outs[0] - lam * outs[1]
        ms = jnp.mean(o * o, axis=-1, keepdims=True)
        o = o * lax.rsqrt(ms + EPS) * gain_ref[:, h * B_V_DIM:(h + 1) * B_V_DIM] * (1.0 - lam_init)
        o_ref[:, h * B_V_DIM:(h + 1) * B_V_DIM] = o.astype(BF16)

    lmin = None
    for h in range(B_HEADS):
        outs = []
        for c in range(2):
            acc = None
            for kc in range(nk):
                p = jnp.exp2(scores(h, c, kc) - bound).astype(BF16)
                pv = _dot(p, v_ref[h, kc * tk:(kc + 1) * tk, :])
                acc = pv if acc is None else acc + pv
            l = acc[:, B_V_DIM:B_V_DIM + 1]
            outs.append(acc[:, 0:B_V_DIM] * (1.0 / l))
            lg = jnp.min(l, axis=0, keepdims=True)
            lmin = lg if lmin is None else jnp.minimum(lmin, lg)
        finish(h, outs)

    @pl.when(jnp.logical_not(lmin[0, 0] >= _L_MIN))
    def _():
        for h in range(B_HEADS):
            outs = []
            for c in range(2):
                m = acc = None
                for kc in range(nk):
                    s = scores(h, c, kc)
                    mc = jnp.max(s, axis=-1, keepdims=True)
                    vc = v_ref[h, kc * tk:(kc + 1) * tk, :]
                    if kc == 0:
                        m = mc
                        acc = _dot(jnp.exp2(s - m).astype(BF16), vc)
                    else:
                        mn = jnp.maximum(m, mc)
                        acc = jnp.exp2(m - mn) * acc + _dot(jnp.exp2(s - mn).astype(BF16), vc)
                        m = mn
                outs.append(acc[:, 0:B_V_DIM] * (1.0 / acc[:, B_V_DIM:B_V_DIM + 1]))
            finish(h, outs)


def _attn_b_tiles(seq):
    return min(256, seq), min(1024, seq)


def _alibi_distance_table(seq):
    tq, _ = _attn_b_tiles(seq)
    r = jnp.arange(tq, dtype=jnp.int32)[:, None]
    u = jnp.arange(2 * seq - tq, dtype=jnp.int32)[None, :]
    return (-LOG2E) * jnp.abs(r - u + (seq - tq)).astype(F32)


def _attn_b(layer, bq, bkt, bv, dist, lam, gain, qk_gains, bsz, seq):
    tokens = bsz * seq
    tq, tk = _attn_b_tiles(seq)
    nq = seq // tq
    return pl.pallas_call(
        functools.partial(_attn_b_kernel, layer, tq, tk, nq, seq),
        grid=(bsz, nq),
        in_specs=[
            pl.BlockSpec((tq, B_QK_WIDTH), lambda b, i: (b * nq + i, 0)),
            pl.BlockSpec((B_QK_WIDTH, seq), lambda b, i: (0, b)),
            pl.BlockSpec((B_HEADS, seq, LANES), lambda b, i: (0, b, 0)),
            pl.BlockSpec(dist.shape, lambda b, i: (0, 0), pipeline_mode=pl.Buffered(1)),
            pl.BlockSpec(lam.shape, lambda b, i: (0, 0)),
            pl.BlockSpec(gain.shape, lambda b, i: (0, 0)),
            pl.BlockSpec(qk_gains.shape, lambda b, i: (0, 0)),
        ],
        out_specs=pl.BlockSpec((tq, B_WIDTH), lambda b, i: (b * nq + i, 0)),
        out_shape=jax.ShapeDtypeStruct((tokens, B_WIDTH), BF16),
        compiler_params=pltpu.CompilerParams(
            dimension_semantics=("arbitrary", "arbitrary"), vmem_limit_bytes=VMEM_LIMIT_BYTES),
        name=f"attn_b{layer}",
    )(bq, bkt, bv, dist, lam, gain, qk_gains)


_C_BLOCK = 16
_C_TILE = 128
_C_NPAIR = C_CHUNK * (C_CHUNK + 1) // 2


def _hgrn_kernel(seq, q_ref, v_ref, lff_ref, lfb_ref, kf_ref, kb_ref, vt_ref, g_ref, r_ref, gain_ref,
                 o_ref,
                 b_scr, dec_scr, qn_scr, kn_scr, tt_scr, oi_scr, st_scr, ot_scr, on_scr):
    nchunk = seq // C_CHUNK
    ntile = seq // _C_TILE
    cpt = _C_TILE // C_CHUNK

    rows_h = lax.broadcasted_iota(jnp.int32, (C_WIDTH, C_WIDTH), 0) // C_VAL_DIM
    cols_h = lax.broadcasted_iota(jnp.int32, (C_WIDTH, C_WIDTH), 1) // C_KEY_DIM
    same_head = rows_h == cols_h
    lane_chunk = lax.broadcasted_iota(jnp.int32, (C_WIDTH, _C_TILE), 1) // C_CHUNK
    row_chunk = lax.broadcasted_iota(jnp.int32, (_C_TILE, C_WIDTH), 0) // C_CHUNK

    nhalf = C_WIDTH // LANES

    def scatter(dst_ref, t, val):
        for c in range(nhalf):
            dst_ref[c, pl.ds(t, nchunk, stride=C_CHUNK), :] = val[:, c * LANES:(c + 1) * LANES]

    def token_rows(src_ref, c0):
        return jnp.concatenate([src_ref[c, pl.ds(c0, _C_TILE), :] for c in range(nhalf)], axis=1)

    for d, (lf_ref, k_ref) in enumerate(((lff_ref, kf_ref), (lfb_ref, kb_ref))):
        pos = (lambda j: j) if d == 0 else (lambda j: C_CHUNK - 1 - j)

        b = None
        for j in range(C_CHUNK):
            lfj = lf_ref[0, pos(j)]
            b = lfj if b is None else b + lfj
            b_scr[j] = b
        bend = b
        dec_scr[...] = jnp.exp(bend)
        for j in range(C_CHUNK):
            bj = b_scr[j]
            scatter(qn_scr, pos(j), q_ref[0, pos(j)] * jnp.exp(bj))
            scatter(kn_scr, pos(j), k_ref[0, pos(j)] * jnp.exp(bend - bj))

        def intra(cb, carry):
            rows = pl.ds(pl.multiple_of(cb * _C_BLOCK, _C_BLOCK), _C_BLOCK)
            pi = 0
            for j in range(C_CHUNK):
                bj = b_scr[j, rows, :]
                qj = q_ref[0, pos(j), rows, :]
                for i in range(j + 1):
                    w = jnp.exp(bj - b_scr[i, rows, :]) * qj * k_ref[0, pos(i), rows, :]
                    tt_scr[pi * _C_BLOCK:(pi + 1) * _C_BLOCK, :] = w.astype(BF16)
                    pi += 1
            y = _dot(tt_scr[...], r_ref[...])
            pi = 0
            for j in range(C_CHUNK):
                acc = None
                for i in range(j + 1):
                    term = y[pi * _C_BLOCK:(pi + 1) * _C_BLOCK, :] * v_ref[0, pos(i), rows, :]
                    acc = term if acc is None else acc + term
                    pi += 1
                if d == 0:
                    oi_scr[pos(j), rows, :] = acc
                else:
                    oi_scr[pos(j), rows, :] = oi_scr[pos(j), rows, :] + acc
            return carry

        lax.fori_loop(0, nchunk // _C_BLOCK, intra, 0)

        st_scr[...] = jnp.zeros_like(st_scr)

        def inter(ti, carry):
            tile = ti if d == 0 else ntile - 1 - ti
            c0 = pl.multiple_of(tile * _C_TILE, _C_TILE)
            kt = token_rows(kn_scr, c0).astype(BF16)
            qt = token_rows(qn_scr, c0).astype(BF16)
            vt = vt_ref[:, pl.ds(c0, _C_TILE)]
            order = list(range(cpt)) if d == 0 else list(range(cpt - 1, -1, -1))
            vexp = jnp.concatenate(
                [jnp.where(lane_chunk == n, vt, jnp.zeros_like(vt)) for n in order], axis=0)
            kv = _dot(vexp, kt)
            st = st_scr[...]
            acc = None
            for nn, n in enumerate(order):
                qm = jnp.where(row_chunk == n, qt, jnp.zeros_like(qt))
                pv = _dot_nt(st.astype(BF16), qm)
                acc = pv if acc is None else acc + pv
                drow = dec_scr[pl.ds(tile * cpt + n, 1), :]
                kvn = kv[nn * C_WIDTH:(nn + 1) * C_WIDTH, :]
                st = st * drow + jnp.where(same_head, kvn, 0.0)
            st_scr[...] = st
            ot_scr[d, :, pl.ds(c0, _C_TILE)] = acc
            return carry

        lax.fori_loop(0, ntile, inter, 0)

    for t in range(C_CHUNK):
        scatter(on_scr, t, oi_scr[t])

    def finish(ti, carry):
        c0 = pl.multiple_of(ti * _C_TILE, _C_TILE)
        o = token_rows(on_scr, c0) + (ot_scr[0, :, pl.ds(c0, _C_TILE)] + ot_scr[1, :, pl.ds(c0, _C_TILE)]).T
        ss = _dot((o * o).astype(BF16), r_ref[...])
        o = o * lax.rsqrt(ss * (1.0 / C_VAL_DIM) + EPS) * gain_ref[...]
        o_ref[pl.ds(c0, _C_TILE), :] = (o * g_ref[pl.ds(c0, _C_TILE), :].astype(F32)).astype(BF16)
        return carry

    lax.fori_loop(0, ntile, finish, 0)


def _hgrn(layer, cq, cv, lff, lfb, kf, kb, cvt, cg, ones_c, gain, bsz, seq):
    tokens = bsz * seq
    nchunk = seq // C_CHUNK
    inter = pl.BlockSpec((1, C_CHUNK, nchunk, C_WIDTH), lambda b: (b, 0, 0, 0))
    return pl.pallas_call(
        functools.partial(_hgrn_kernel, seq),
        grid=(bsz,),
        in_specs=[
            inter, inter, inter, inter, inter, inter,
            pl.BlockSpec((C_WIDTH, seq), lambda b: (0, b)),
            pl.BlockSpec((seq, C_WIDTH), lambda b: (b, 0)),
            pl.BlockSpec((C_WIDTH, C_WIDTH), lambda b: (0, 0)),
            pl.BlockSpec((1, C_WIDTH), lambda b: (0, 0)),
        ],
        out_specs=pl.BlockSpec((seq, C_WIDTH), lambda b: (b, 0)),
        out_shape=jax.ShapeDtypeStruct((tokens, C_WIDTH), BF16),
        scratch_shapes=[
            pltpu.VMEM((C_CHUNK, nchunk, C_WIDTH), F32),
            pltpu.VMEM((nchunk, C_WIDTH), F32),
            pltpu.VMEM((C_WIDTH // LANES, seq, LANES), F32),
            pltpu.VMEM((C_WIDTH // LANES, seq, LANES), F32),
            pltpu.VMEM((_C_NPAIR * _C_BLOCK, C_WIDTH), BF16),
            pltpu.VMEM((C_CHUNK, nchunk, C_WIDTH), F32),
            pltpu.VMEM((C_WIDTH, C_WIDTH), F32),
            pltpu.VMEM((2, C_WIDTH, seq), F32),
            pltpu.VMEM((C_WIDTH // LANES, seq, LANES), F32),
        ],
        compiler_params=pltpu.CompilerParams(
            dimension_semantics=("arbitrary",), vmem_limit_bytes=VMEM_LIMIT_BYTES),
        name=f"hgrn{layer}",
    )(cq, cv, lff, lfb, kf, kb, cvt, cg, ones_c, gain)


_F_BLOCK = 256
_F_ROWS = 512


def _gelu(x):
    return 0.5 * x * (1.0 + lax.erf(x * (2.0 ** -0.5)))


def _ffn_kernel(seq, nsteps, x_ref, oa_ref, ob_ref, oc_ref, wo_ref, fn_ref,
                wgf_ref, wvf_ref, wg0_ref, wv0_ref, wg1_ref, wv1_ref,
                cw0_ref, cb0_ref, wd0_ref, cw1_ref, cb1_ref, wd1_ref, cwl_ref, cbl_ref, wdl_ref,
                o_ref, h_scr, g_scr, v_scr):
    i = pl.program_id(1)
    rb = min(_F_ROWS, seq)
    halo = 8

    def up(slot, r, wg_ref, wv_ref):
        rows = slice(r * rb, (r + 1) * rb)
        h = h_scr[rows, :]
        g_scr[slot, rows, :] = _dot(h, wg_ref[...])
        v_scr[slot, rows, :] = _dot(h, wv_ref[...])

    def consume(slot, r, cw_ref, cb_ref, wd_ref):
        lo = r * rb
        a, b = max(lo - halo, 0), min(lo + rb + halo, seq)
        n, off = b - a, lo - a
        g = g_scr[slot, a:b, :]
        row = lax.broadcasted_iota(jnp.int32, g.shape, 0) + a
        prev = jnp.where(row == 0, 0.0, pltpu.roll(g, 1, 0))[off:off + rb]
        nxt = jnp.where(row == seq - 1, 0.0, pltpu.roll(g, n - 1, 0))[off:off + rb]
        cw = cw_ref[...]
        conv = prev * cw[0:1, :] + g[off:off + rb] * cw[1:2, :] + nxt * cw[2:3, :] + cb_ref[...]
        act = (_gelu(conv) * v_scr[slot, lo:lo + rb, :]).astype(BF16)
        o_ref[lo:lo + rb, :] += _dot(act, wd_ref[...])

    nrb = seq // rb

    @pl.when(i == 0)
    def _():
        for r in range(nrb):
            rows = slice(r * rb, (r + 1) * rb)
            mix = jnp.concatenate([oa_ref[rows, :], ob_ref[rows, :], oc_ref[rows, :]], axis=1)
            xm = x_ref[rows, :] + _dot(mix, wo_ref[...])
            o_ref[rows, :] = xm
            ms = jnp.mean(xm * xm, axis=-1, keepdims=True)
            h_scr[rows, :] = (xm * lax.rsqrt(ms + EPS) * fn_ref[...]).astype(BF16)
            up(0, r, wgf_ref, wvf_ref)

    for r in range(nrb):
        consume(0, r, cw0_ref, cb0_ref, wd0_ref)
        up(1, r, wg0_ref, wv0_ref)
    for r in range(nrb):
        consume(1, r, cw1_ref, cb1_ref, wd1_ref)
        up(0, r, wg1_ref, wv1_ref)

    @pl.when(i == nsteps - 1)
    def _():
        for r in range(nrb):
            consume(0, r, cwl_ref, cbl_ref, wdl_ref)


def _ffn(layer, xf, oa, ob, oc, wo, fn, wup, cw, cb, wd, bsz, seq):
    tokens = bsz * seq
    nf = D_FF // _F_BLOCK
    assert D_FF % _F_BLOCK == 0 and nf % 2 == 1
    nsteps = (nf - 1) // 2
    single = pl.Buffered(1)
    up_spec = lambda blk: [pl.BlockSpec((D_MODEL, _F_BLOCK), lambda b, i: (0, blk(i))),
                           pl.BlockSpec((D_MODEL, _F_BLOCK), lambda b, i: (0, blk(i) + nf))]
    down_spec = lambda blk: [pl.BlockSpec((3, _F_BLOCK), lambda b, i: (0, blk(i))),
                             pl.BlockSpec((1, _F_BLOCK), lambda b, i: (0, blk(i))),
                             pl.BlockSpec((_F_BLOCK, D_MODEL), lambda b, i: (blk(i), 0))]
    return pl.pallas_call(
        functools.partial(_ffn_kernel, seq, nsteps),
        grid=(bsz, nsteps),
        in_specs=[
            pl.BlockSpec((seq, D_MODEL), lambda b, i: (b, 0), pipeline_mode=single),
            pl.BlockSpec((seq, A_WIDTH), lambda b, i: (b, 0), pipeline_mode=single),
            pl.BlockSpec((seq, B_WIDTH), lambda b, i: (b, 0), pipeline_mode=single),
            pl.BlockSpec((seq, C_WIDTH), lambda b, i: (b, 0), pipeline_mode=single),
            pl.BlockSpec((D_MIX, D_MODEL), lambda b, i: (0, 0), pipeline_mode=single),
            pl.BlockSpec((1, D_MODEL), lambda b, i: (0, 0)),
            *up_spec(lambda i: 0), *up_spec(lambda i: 2 * i + 1), *up_spec(lambda i: 2 * i + 2),
            *down_spec(lambda i: 2 * i), *down_spec(lambda i: 2 * i + 1), *down_spec(lambda i: nf - 1),
        ],
        out_specs=pl.BlockSpec((seq, D_MODEL), lambda b, i: (b, 0)),
        out_shape=jax.ShapeDtypeStruct((tokens, D_MODEL), F32),
        scratch_shapes=[
            pltpu.VMEM((seq, D_MODEL), BF16),
            pltpu.VMEM((2, seq, _F_BLOCK), F32),
            pltpu.VMEM((2, seq, _F_BLOCK), F32),
        ],
        compiler_params=pltpu.CompilerParams(
            dimension_semantics=("arbitrary", "arbitrary"), vmem_limit_bytes=VMEM_LIMIT_BYTES),
        name=f"ffn{layer}",
    )(xf, oa, ob, oc, wo, fn, wup, wup, wup, wup, wup, wup, cw, cb, wd, cw, cb, wd, cw, cb, wd)


def _rope_tables(seq):
    rows = seq // GRID_W
    row = jnp.repeat(jnp.arange(rows), GRID_W)
    col = jnp.tile(jnp.arange(GRID_W), rows)
    inv_freq = ROPE_THETA ** (-jnp.arange(0, A_AXIS_DIM, 2, dtype=F32) / A_AXIS_DIM)
    ang = jnp.stack([row, col], axis=-1).astype(F32)[..., None] * inv_freq
    cos, sin = jnp.cos(ang), jnp.sin(ang)
    cos_h = jnp.concatenate([cos[:, 0], cos[:, 0], cos[:, 1], cos[:, 1]], axis=-1)
    sin_h = jnp.concatenate([-sin[:, 0], sin[:, 0], -sin[:, 1], sin[:, 1]], axis=-1)
    return jnp.tile(cos_h, (1, A_HEADS)), jnp.tile(sin_h, (1, A_HEADS))


def _partner_index(width):
    half = A_AXIS_DIM // 2
    i = np.arange(width)
    return np.where((i % A_AXIS_DIM) < half, i + half, i - half)


def _block_ones(width, block):
    i = np.arange(width) // block
    return jnp.asarray(i[:, None] == i[None, :], BF16)


def kernel(x, attn_norm, w_in, a_q_norm, a_k_norm, b_q_norm, b_k_norm, b_lambda, b_sub_norm,
           c_lb_logits, c_out_norm, w_out, ffn_norm, w_up, conv_w, conv_b, w_down):
    bsz, seq, _ = x.shape
    tokens = bsz * seq
    assert seq % 256 == 0 and seq % GRID_W == 0

    cos, sin = _rope_tables(seq)
    ones_a = _block_ones(A_WIDTH, A_HEAD_DIM)
    ones_b = _block_ones(B_QK_WIDTH, B_HEAD_DIM)
    ones_c = _block_ones(C_WIDTH, C_VAL_DIM)
    pq = _partner_index(A_WIDTH)
    pk = _partner_index(A_KV_WIDTH)

    dist = _alibi_distance_table(seq)

    xf = x.reshape(tokens, D_MODEL)
    for l in range(DEPTH):
        w = w_in[l]
        wq, wk, wv = (w[:, _OFF[0]:_OFF[1]], w[:, _OFF[1]:_OFF[2]], w[:, _OFF[2]:_OFF[3]])
        wa = jnp.concatenate([wq, wq[:, pq], wk, wk[:, pk], wv], axis=1).astype(BF16)
        wb = w[:, _OFF[3]:_OFF[6]].astype(BF16)
        wc = w[:, _OFF[6]:_OFF[11]].astype(BF16)
        gq = jnp.tile(a_q_norm[l], A_HEADS)
        gk = jnp.pad(jnp.tile(a_k_norm[l], A_KV_HEADS), (0, A_WIDTH - A_KV_WIDTH))
        gkp = jnp.pad(jnp.tile(a_k_norm[l], A_KV_HEADS)[pk], (0, A_WIDTH - A_KV_WIDTH))
        ga = jnp.stack([gq, gq[pq], gk, gkp])
        gb = jnp.stack([jnp.tile(b_q_norm[l], 2 * B_HEADS), jnp.tile(b_k_norm[l], 2 * B_HEADS)])

        (aq, akt, av, bq, bkt, bv, cq, cv, lff, lfb, kf, kb, cg, cvt) = _proj(
            l, xf, attn_norm[l][None, :], wa, wb, wc, ga, gb, c_lb_logits, cos, sin, ones_a, ones_b,
            bsz, seq)
        oa = _attn_a(l, aq, akt, av, jnp.stack([a_q_norm[l], a_k_norm[l]]), bsz, seq)
        ob = _attn_b(l, bq, bkt, bv, dist, b_lambda[l], b_sub_norm[l][None, :],
                     jnp.stack([b_q_norm[l], b_k_norm[l]]), bsz, seq)
        oc = _hgrn(l, cq, cv, lff, lfb, kf, kb, cvt, cg, ones_c, c_out_norm[l][None, :], bsz, seq)
        xf = _ffn(l, xf, oa, ob, oc, w_out[l].astype(BF16), ffn_norm[l][None, :],
                  w_up[l].astype(BF16), conv_w[l], conv_b[l][None, :], w_down[l].astype(BF16), bsz, seq)
    return xf.reshape(bsz, seq, D_MODEL)
```

```python
import functools
import math

import jax
import jax.numpy as jnp
import numpy as np
from jax import lax
from jax.experimental import pallas as pl
from jax.experimental.pallas import tpu as pltpu

D_MODEL = 1024
DEPTH = 4
GRID_W = 64
EPS = 1e-6
MIN_FORGET = 1e-20
LOG2E = 1.4426950408889634

A_HEADS, A_KV_HEADS, A_HEAD_DIM = 6, 2, 64
A_GROUP = A_HEADS // A_KV_HEADS
A_AXIS_DIM = A_HEAD_DIM // 2
ROPE_THETA = 10000.0
A_WIDTH = A_HEADS * A_HEAD_DIM
A_KV_WIDTH = A_KV_HEADS * A_HEAD_DIM

B_HEADS, B_HEAD_DIM = 4, 48
B_V_DIM = 2 * B_HEAD_DIM
B_QK_WIDTH = B_HEADS * 2 * B_HEAD_DIM
B_WIDTH = B_HEADS * B_V_DIM

C_HEADS, C_KEY_DIM, C_VAL_DIM = 4, 64, 64
C_WIDTH = C_HEADS * C_KEY_DIM
C_CHUNK = 16

D_MIX = A_WIDTH + B_WIDTH + C_WIDTH
D_FF = 2816

VMEM_LIMIT_BYTES = 56 * 1024 * 1024

_OFF = np.cumsum([0, A_WIDTH, A_KV_WIDTH, A_KV_WIDTH, B_QK_WIDTH, B_QK_WIDTH, B_WIDTH,
                  C_WIDTH, C_WIDTH, C_WIDTH, C_WIDTH, C_WIDTH])

BF16 = jnp.bfloat16
F32 = jnp.float32
LANES = 128


def _dot(a, b):
    return jnp.dot(a, b, preferred_element_type=F32)


def _dot_nt(a, b):
    return lax.dot_general(a, b, (((1,), (1,)), ((), ())), preferred_element_type=F32)


def _sigmoid(z):
    return 1.0 / (1.0 + jnp.exp(-z))


def _proj_kernel(layer, tm,
                 x_ref, an_ref, wa_ref, wb_ref, wc_ref, ga_ref, gb_ref, lbl_ref,
                 cos_ref, sin_ref, ones_a_ref, ones_b_ref,
                 aq_ref, akt_ref, av_ref, bq_ref, bkt_ref, bv_ref,
                 cq_ref, cv_ref, lff_ref, lfb_ref, cg_ref, cvt_ref,
                 scr_ref):
    x = x_ref[...]
    ms = jnp.mean(x * x, axis=-1, keepdims=True)
    h = (x * lax.rsqrt(ms + EPS) * an_ref[...]).astype(BF16)

    cos = cos_ref[...]
    sin = sin_ref[...]
    nchunk = tm // C_CHUNK

    def norm_rope(y, yp, width, g, gp, scale):
        ones = ones_a_ref[0:width, 0:width]
        ss = _dot((y * y).astype(BF16), ones)
        r = lax.rsqrt(ss * (1.0 / A_HEAD_DIM) + EPS)
        out = (y * g * cos[:, 0:width] + yp * gp * sin[:, 0:width]) * r
        return out * scale if scale != 1.0 else out

    def norm_b(y, g, scale):
        ss = _dot((y * y).astype(BF16), ones_b_ref[...])
        r = lax.rsqrt(ss * (1.0 / B_HEAD_DIM) + EPS)
        out = y * g * r
        return out * scale if scale != 1.0 else out

    def with_ones_column(v):
        tail = (lax.broadcasted_iota(jnp.int32, (tm, LANES - v.shape[1]), 1) == 0).astype(F32)
        return jnp.concatenate([v, tail], axis=1).astype(BF16)

    def interleave(slot, val, out_ref):
        for c in range(C_WIDTH // LANES):
            scr_ref[slot, c] = val[:, c * LANES:(c + 1) * LANES]
        for t in range(C_CHUNK):
            for c in range(C_WIDTH // LANES):
                out_ref[0, t, :, c * LANES:(c + 1) * LANES] = scr_ref[slot, c, pl.ds(t, nchunk, stride=C_CHUNK), :]

    def forget_lower_bound():
        logits = lbl_ref[...]
        mx = jnp.max(logits, axis=0, keepdims=True)
        e = jnp.exp(logits - mx)
        p = e / jnp.sum(e, axis=0, keepdims=True)
        cum = p[0:1, :]
        for i in range(1, layer + 1):
            cum = cum + p[i:i + 1, :]
        return jnp.maximum(cum - p[0:1, :], 0.0)

    def epi_aq(y):
        q = norm_rope(y[:, 0:A_WIDTH], y[:, A_WIDTH:2 * A_WIDTH], A_WIDTH,
                      ga_ref[0:1, 0:A_WIDTH], ga_ref[1:2, 0:A_WIDTH], A_HEAD_DIM ** -0.5 * LOG2E)
        aq_ref[...] = q.astype(BF16)

    def epi_ak(y):
        k = norm_rope(y[:, 0:A_KV_WIDTH], y[:, A_KV_WIDTH:2 * A_KV_WIDTH], A_KV_WIDTH,
                      ga_ref[2:3, 0:A_KV_WIDTH], ga_ref[3:4, 0:A_KV_WIDTH], 1.0)
        akt_ref[...] = k.T.astype(BF16)

    def epi_av(y):
        for j in range(A_KV_HEADS):
            av_ref[j] = with_ones_column(y[:, j * A_HEAD_DIM:(j + 1) * A_HEAD_DIM])

    def epi_bq(y):
        bq_ref[...] = norm_b(y, gb_ref[0:1, :], B_HEAD_DIM ** -0.5 * LOG2E).astype(BF16)

    def epi_bk(y):
        bkt_ref[...] = norm_b(y, gb_ref[1:2, :], 1.0).T.astype(BF16)

    def epi_bv(y):
        for j in range(B_HEADS):
            bv_ref[j] = with_ones_column(y[:, j * B_V_DIM:(j + 1) * B_V_DIM])

    def epi_cq(y):
        interleave(0, y * _sigmoid(y), cq_ref)

    def epi_forget(slot, lf_ref):
        def epi(z):
            lb = forget_lower_bound()
            f = jnp.maximum(lb + (1.0 - lb) * _sigmoid(z), MIN_FORGET)
            interleave(slot, jnp.log(f), lf_ref)
        return epi

    def epi_cv(y):
        interleave(3, y, cv_ref)
        cvt_ref[...] = y.T.astype(BF16)

    def epi_cg(y):
        cg_ref[...] = (y * _sigmoid(y)).astype(BF16)

    oa_k = 2 * A_WIDTH
    oa_v = oa_k + 2 * A_KV_WIDTH
    groups = [
        (wa_ref, 0, 2 * A_WIDTH, epi_aq),
        (wa_ref, oa_k, 2 * A_KV_WIDTH, epi_ak),
        (wa_ref, oa_v, A_KV_WIDTH, epi_av),
        (wb_ref, 0, B_QK_WIDTH, epi_bq),
        (wb_ref, B_QK_WIDTH, B_QK_WIDTH, epi_bk),
        (wb_ref, 2 * B_QK_WIDTH, B_WIDTH, epi_bv),
        (wc_ref, 0, C_WIDTH, epi_cq),
        (wc_ref, C_WIDTH, C_WIDTH, epi_forget(1, lff_ref)),
        (wc_ref, 2 * C_WIDTH, C_WIDTH, epi_forget(2, lfb_ref)),
        (wc_ref, 3 * C_WIDTH, C_WIDTH, epi_cv),
        (wc_ref, 4 * C_WIDTH, C_WIDTH, epi_cg),
    ]
    pending = None
    for w_ref, lo, width, epi in groups:
        y = _dot(h, w_ref[:, lo:lo + width])
        if pending is not None:
            pending[0](pending[1])
        pending = (epi, y)
    pending[0](pending[1])


def _proj(layer, xf, an, wa, wb, wc, ga, gb, lbl, cos, sin, ones_a, ones_b, bsz, seq):
    tokens = bsz * seq
    tm = min(512, seq)
    nt = seq // tm
    nchunk = tm // C_CHUNK
    const = lambda shape: pl.BlockSpec(shape, lambda i: (0,) * len(shape))
    tok = lambda w: pl.BlockSpec((tm, w), lambda i: (i, 0))
    tr = lambda w: pl.BlockSpec((w, tm), lambda i: (0, i))
    head = lambda nh, w: pl.BlockSpec((nh, tm, w), lambda i: (0, i, 0))
    inter = pl.BlockSpec((1, C_CHUNK, nchunk, C_WIDTH), lambda i: (i // nt, 0, i % nt, 0))
    inter_shape = jax.ShapeDtypeStruct((bsz, C_CHUNK, seq // C_CHUNK, C_WIDTH), F32)
    out_shape = (
        jax.ShapeDtypeStruct((tokens, A_WIDTH), BF16),
        jax.ShapeDtypeStruct((A_KV_WIDTH, tokens), BF16),
        jax.ShapeDtypeStruct((A_KV_HEADS, tokens, LANES), BF16),
        jax.ShapeDtypeStruct((tokens, B_QK_WIDTH), BF16),
        jax.ShapeDtypeStruct((B_QK_WIDTH, tokens), BF16),
        jax.ShapeDtypeStruct((B_HEADS, tokens, LANES), BF16),
        inter_shape, inter_shape, inter_shape, inter_shape,
        jax.ShapeDtypeStruct((tokens, C_WIDTH), BF16),
        jax.ShapeDtypeStruct((C_WIDTH, tokens), BF16),
    )
    out_specs = (
        tok(A_WIDTH), tr(A_KV_WIDTH), head(A_KV_HEADS, LANES),
        tok(B_QK_WIDTH), tr(B_QK_WIDTH), head(B_HEADS, LANES),
        inter, inter, inter, inter,
        tok(C_WIDTH), tr(C_WIDTH),
    )
    in_specs = [
        tok(D_MODEL), const((1, D_MODEL)),
        const(wa.shape), const(wb.shape), const(wc.shape),
        const(ga.shape), const(gb.shape), const(lbl.shape),
        pl.BlockSpec((tm, A_WIDTH), lambda i: (i % nt, 0)),
        pl.BlockSpec((tm, A_WIDTH), lambda i: (i % nt, 0)),
        const(ones_a.shape), const(ones_b.shape),
    ]
    return pl.pallas_call(
        functools.partial(_proj_kernel, layer, tm),
        grid=(tokens // tm,),
        in_specs=in_specs,
        out_specs=out_specs,
        out_shape=out_shape,
        scratch_shapes=[pltpu.VMEM((4, C_WIDTH // LANES, tm, LANES), F32)],
        compiler_params=pltpu.CompilerParams(
            dimension_semantics=("arbitrary",), vmem_limit_bytes=VMEM_LIMIT_BYTES),
        name=f"proj{layer}",
    )(xf, an, wa, wb, wc, ga, gb, lbl, cos, sin, ones_a, ones_b)


_L_MIN = 2.0 ** -100
_BOUND_SLACK = 1.02


def _score_bound(g_ref, dim):
    g = jnp.abs(g_ref[...])
    gq = jnp.max(g[0:1, :], axis=-1, keepdims=True)
    gk = jnp.max(g[1:2, :], axis=-1, keepdims=True)
    return gq * gk * (dim ** 0.5 * LOG2E * _BOUND_SLACK)


def _attn_a_kernel(tq, tk, q_ref, kt_ref, v_ref, g_ref, o_ref):
    seq = kt_ref.shape[1]
    nk = seq // tk
    bound = _score_bound(g_ref, A_HEAD_DIM)

    def group_q(g):
        heads = [A_GROUP * g + j for j in range(A_GROUP)]
        return heads, jnp.concatenate(
            [q_ref[:, h * A_HEAD_DIM:(h + 1) * A_HEAD_DIM] for h in heads], axis=0)

    def scores(qg, g, c):
        return _dot(qg, kt_ref[g * A_HEAD_DIM:(g + 1) * A_HEAD_DIM, c * tk:(c + 1) * tk])

    def store(g, heads, o):
        for j, h in enumerate(heads):
            o_ref[:, h * A_HEAD_DIM:(h + 1) * A_HEAD_DIM] = o[j * tq:(j + 1) * tq, 0:A_HEAD_DIM].astype(BF16)

    lmin = None
    for g in range(A_KV_HEADS):
        heads, qg = group_q(g)
        acc = None
        for c in range(nk):
            p = jnp.exp2(scores(qg, g, c) - bound).astype(BF16)
            pv = _dot(p, v_ref[g, c * tk:(c + 1) * tk, :])
            acc = pv if acc is None else acc + pv
        l = acc[:, A_HEAD_DIM:A_HEAD_DIM + 1]
        store(g, heads, acc * (1.0 / l))
        lg = jnp.min(l, axis=0, keepdims=True)
        lmin = lg if lmin is None else jnp.minimum(lmin, lg)

    @pl.when(jnp.logical_not(lmin[0, 0] >= _L_MIN))
    def _():
        for g in range(A_KV_HEADS):
            heads, qg = group_q(g)
            m = acc = None
            for c in range(nk):
                s = scores(qg, g, c)
                mc = jnp.max(s, axis=-1, keepdims=True)
                vc = v_ref[g, c * tk:(c + 1) * tk, :]
                if c == 0:
                    m = mc
                    acc = _dot(jnp.exp2(s - m).astype(BF16), vc)
                else:
                    mn = jnp.maximum(m, mc)
                    acc = jnp.exp2(m - mn) * acc + _dot(jnp.exp2(s - mn).astype(BF16), vc)
                    m = mn
            store(g, heads, acc * (1.0 / acc[:, A_HEAD_DIM:A_HEAD_DIM + 1]))


def _attn_a(layer, aq, akt, av, gains, bsz, seq):
    tokens = bsz * seq
    tq = min(512, seq)
    tk = min(256, seq)
    nq = seq // tq
    return pl.pallas_call(
        functools.partial(_attn_a_kernel, tq, tk),
        grid=(bsz, nq),
        in_specs=[
            pl.BlockSpec((tq, A_WIDTH), lambda b, i: (b * nq + i, 0)),
            pl.BlockSpec((A_KV_WIDTH, seq), lambda b, i: (0, b)),
            pl.BlockSpec((A_KV_HEADS, seq, LANES), lambda b, i: (0, b, 0)),
            pl.BlockSpec(gains.shape, lambda b, i: (0, 0)),
        ],
        out_specs=pl.BlockSpec((tq, A_WIDTH), lambda b, i: (b * nq + i, 0)),
        out_shape=jax.ShapeDtypeStruct((tokens, A_WIDTH), BF16),
        compiler_params=pltpu.CompilerParams(
            dimension_semantics=("arbitrary", "arbitrary"), vmem_limit_bytes=VMEM_LIMIT_BYTES),
        name=f"attn_a{layer}",
    )(aq, akt, av, gains)


def _attn_b_kernel(layer, tq, tk, nq, seq, q_ref, kt_ref, v_ref, dist_ref, lam_ref, gain_ref, g_ref, o_ref):
    i = pl.program_id(1)
    lam_p = lam_ref[...]
    lam_init = 0.8 - 0.6 * math.exp(-0.3 * layer)
    lam = (jnp.exp(jnp.sum(lam_p[0:1] * lam_p[1:2], axis=-1, keepdims=True))
           - jnp.exp(jnp.sum(lam_p[2:3] * lam_p[3:4], axis=-1, keepdims=True)) + lam_init)
    off = pl.multiple_of((nq - 1 - i) * tq, tq)
    nk = seq // tk
    bound = _score_bound(g_ref, B_HEAD_DIM)

    def scores(h, c, kc):
        lo = (2 * h + c) * B_HEAD_DIM
        slope = 2.0 ** (-8.0 * (h + 1) / B_HEADS)
        bias = dist_ref[:, pl.ds(off + kc * tk, tk)] * slope
        return _dot(q_ref[:, lo:lo + B_HEAD_DIM], kt_ref[lo:lo + B_HEAD_DIM, kc * tk:(kc + 1) * tk]) + bias

    def finish(h, outs):
        o = outs[0] - lam * outs[1]
        ms = jnp.mean(o * o, axis=-1, keepdims=True)
        o = o * lax.rsqrt(ms + EPS) * gain_ref[:, h * B_V_DIM:(h + 1) * B_V_DIM] * (1.0 - lam_init)
        o_ref[:, h * B_V_DIM:(h + 1) * B_V_DIM] = o.astype(BF16)

    lmin = None
    for h in range(B_HEADS):
        outs = []
        for c in range(2):
            acc = None
            for kc in range(nk):
                p = jnp.exp2(scores(h, c, kc) - bound).astype(BF16)
                pv = _dot(p, v_ref[h, kc * tk:(kc + 1) * tk, :])
                acc = pv if acc is None else acc + pv
            l = acc[:, B_V_DIM:B_V_DIM + 1]
            outs.append(acc[:, 0:B_V_DIM] * (1.0 / l))
            lg = jnp.min(l, axis=0, keepdims=True)
            lmin = lg if lmin is None else jnp.minimum(lmin, lg)
        finish(h, outs)

    @pl.when(jnp.logical_not(lmin[0, 0] >= _L_MIN))
    def _():
        for h in range(B_HEADS):
            outs = []
            for c in range(2):
                m = acc = None
                for kc in range(nk):
                    s = scores(h, c, kc)
                    mc = jnp.max(s, axis=-1, keepdims=True)
                    vc = v_ref[h, kc * tk:(kc + 1) * tk, :]
                    if kc == 0:
                        m = mc
                        acc = _dot(jnp.exp2(s - m).astype(BF16), vc)
                    else:
                        mn = jnp.maximum(m, mc)
                        acc = jnp.exp2(m - mn) * acc + _dot(jnp.exp2(s - mn).astype(BF16), vc)
                        m = mn
                outs.append(acc[:, 0:B_V_DIM] * (1.0 / acc[:, B_V_DIM:B_V_DIM + 1]))
            finish(h, outs)


def _attn_b_tiles(seq):
    return min(256, seq), min(1024, seq)


def _alibi_distance_table(seq):
    tq, _ = _attn_b_tiles(seq)
    r = jnp.arange(tq, dtype=jnp.int32)[:, None]
    u = jnp.arange(2 * seq - tq, dtype=jnp.int32)[None, :]
    return (-LOG2E) * jnp.abs(r - u + (seq - tq)).astype(F32)


def _attn_b(layer, bq, bkt, bv, dist, lam, gain, qk_gains, bsz, seq):
    tokens = bsz * seq
    tq, tk = _attn_b_tiles(seq)
    nq = seq // tq
    return pl.pallas_call(
        functools.partial(_attn_b_kernel, layer, tq, tk, nq, seq),
        grid=(bsz, nq),
        in_specs=[
            pl.BlockSpec((tq, B_QK_WIDTH), lambda b, i: (b * nq + i, 0)),
            pl.BlockSpec((B_QK_WIDTH, seq), lambda b, i: (0, b)),
            pl.BlockSpec((B_HEADS, seq, LANES), lambda b, i: (0, b, 0)),
            pl.BlockSpec(dist.shape, lambda b, i: (0, 0), pipeline_mode=pl.Buffered(1)),
            pl.BlockSpec(lam.shape, lambda b, i: (0, 0)),
            pl.BlockSpec(gain.shape, lambda b, i: (0, 0)),
            pl.BlockSpec(qk_gains.shape, lambda b, i: (0, 0)),
        ],
        out_specs=pl.BlockSpec((tq, B_WIDTH), lambda b, i: (b * nq + i, 0)),
        out_shape=jax.ShapeDtypeStruct((tokens, B_WIDTH), BF16),
        compiler_params=pltpu.CompilerParams(
            dimension_semantics=("arbitrary", "arbitrary"), vmem_limit_bytes=VMEM_LIMIT_BYTES),
        name=f"attn_b{layer}",
    )(bq, bkt, bv, dist, lam, gain, qk_gains)


_C_BLOCK = 16
_C_TILE = 128
_C_NPAIR = C_CHUNK * (C_CHUNK + 1) // 2
_C_PAIR_GROUPS = ((0, 8), (8, 11), (11, 14), (14, 16))


def _hgrn_kernel(seq, q_ref, v_ref, lff_ref, lfb_ref, vt_ref, g_ref, r_ref, gain_ref,
                 o_ref,
                 b_scr, k_scr, dec_scr, qn_scr, kn_scr, tt_scr, oi_scr, st_scr, kv_scr, ot_scr, on_scr):
    nchunk = seq // C_CHUNK
    ntile = seq // _C_TILE
    cpt = _C_TILE // C_CHUNK

    rows_h = lax.broadcasted_iota(jnp.int32, (C_WIDTH, C_WIDTH), 0) // C_VAL_DIM
    cols_h = lax.broadcasted_iota(jnp.int32, (C_WIDTH, C_WIDTH), 1) // C_KEY_DIM
    same_head = rows_h == cols_h
    lane_chunk = lax.broadcasted_iota(jnp.int32, (C_WIDTH, _C_TILE), 1) // C_CHUNK
    row_chunk = lax.broadcasted_iota(jnp.int32, (_C_TILE, C_WIDTH), 0) // C_CHUNK

    nhalf = C_WIDTH // LANES

    def scatter(dst_ref, t, val):
        for c in range(nhalf):
            dst_ref[c, pl.ds(t, nchunk, stride=C_CHUNK), :] = val[:, c * LANES:(c + 1) * LANES]

    def token_rows(src_ref, c0):
        return jnp.concatenate([src_ref[c, pl.ds(c0, _C_TILE), :] for c in range(nhalf)], axis=1)

    groups = _C_PAIR_GROUPS
    offs = [sum(j + 1 for j in range(groups[g][0])) for g in range(len(groups))]

    for d, lf_ref in enumerate((lff_ref, lfb_ref)):
        pos = (lambda j: j) if d == 0 else (lambda j: C_CHUNK - 1 - j)
        order = list(range(cpt)) if d == 0 else list(range(cpt - 1, -1, -1))
        tile_at = (lambda s: s) if d == 0 else (lambda s: ntile - 1 - s)

        b = None
        for j in range(C_CHUNK):
            lfj = lf_ref[0, pos(j)]
            k_scr[pos(j)] = 1.0 - jnp.exp(lfj)
            b = lfj if b is None else b + lfj
            b_scr[j] = b
        bend = b
        dec_scr[...] = jnp.exp(bend)
        for j in range(C_CHUNK):
            bj = b_scr[j]
            scatter(qn_scr, pos(j), q_ref[0, pos(j)] * jnp.exp(bj))
            scatter(kn_scr, pos(j), k_scr[pos(j)] * jnp.exp(bend - bj))

        def intra(cb, carry):
            rows = pl.ds(pl.multiple_of(cb * _C_BLOCK, _C_BLOCK), _C_BLOCK)

            def build(g):
                pi = offs[g]
                for j in range(*groups[g]):
                    bj = b_scr[j, rows, :]
                    qj = q_ref[0, pos(j), rows, :]
                    for i in range(j + 1):
                        w = jnp.exp(bj - b_scr[i, rows, :]) * qj * k_scr[pos(i), rows, :]
                        tt_scr[pi * _C_BLOCK:(pi + 1) * _C_BLOCK, :] = w.astype(BF16)
                        pi += 1
                return _dot(tt_scr[offs[g] * _C_BLOCK:pi * _C_BLOCK, :], r_ref[...])

            def consume(g, y):
                pi = 0
                for j in range(*groups[g]):
                    acc = None
                    for i in range(j + 1):
                        term = y[pi * _C_BLOCK:(pi + 1) * _C_BLOCK, :] * v_ref[0, pos(i), rows, :]
                        acc = term if acc is None else acc + term
                        pi += 1
                    if d == 0:
                        oi_scr[pos(j), rows, :] = acc
                    else:
                        oi_scr[pos(j), rows, :] = oi_scr[pos(j), rows, :] + acc

            prev = None
            for g in range(len(groups)):
                y = build(g)
                if prev is not None:
                    consume(*prev)
                prev = (g, y)
            consume(*prev)
            return carry

        lax.fori_loop(0, nchunk // _C_BLOCK, intra, 0)

        def kv_products(slot, tile):
            c0 = pl.multiple_of(tile * _C_TILE, _C_TILE)
            kt = token_rows(kn_scr, c0).astype(BF16)
            vt = vt_ref[:, pl.ds(c0, _C_TILE)]
            vexp = jnp.concatenate(
                [jnp.where(lane_chunk == n, vt, jnp.zeros_like(vt)) for n in order], axis=0)
            kv_scr[slot] = _dot(vexp, kt)

        def chain(slot, tile):
            c0 = pl.multiple_of(tile * _C_TILE, _C_TILE)
            qt = token_rows(qn_scr, c0).astype(BF16)
            st = st_scr[...]
            acc = None
            for nn, n in enumerate(order):
                qm = jnp.where(row_chunk == n, qt, jnp.zeros_like(qt))
                pv = _dot_nt(st.astype(BF16), qm)
                acc = pv if acc is None else acc + pv
                drow = dec_scr[pl.ds(tile * cpt + n, 1), :]
                kvn = kv_scr[slot, nn * C_WIDTH:(nn + 1) * C_WIDTH, :]
                st = st * drow + jnp.where(same_head, kvn, 0.0)
            st_scr[...] = st
            ot_scr[d, :, pl.ds(c0, _C_TILE)] = acc

        st_scr[...] = jnp.zeros_like(st_scr)
        kv_products(0, tile_at(0))

        def inter(it, carry):
            s0 = 2 * it
            kv_products(1, tile_at(s0 + 1))
            chain(0, tile_at(s0))
            kv_products(0, tile_at(jnp.minimum(s0 + 2, ntile - 1)))
            chain(1, tile_at(s0 + 1))
            return carry

        lax.fori_loop(0, ntile // 2, inter, 0)

    for t in range(C_CHUNK):
        scatter(on_scr, t, oi_scr[t])

    def finish(ti, carry):
        c0 = pl.multiple_of(ti * _C_TILE, _C_TILE)
        o = token_rows(on_scr, c0) + (ot_scr[0, :, pl.ds(c0, _C_TILE)] + ot_scr[1, :, pl.ds(c0, _C_TILE)]).T
        ss = _dot((o * o).astype(BF16), r_ref[...])
        o = o * lax.rsqrt(ss * (1.0 / C_VAL_DIM) + EPS) * gain_ref[...]
        o_ref[pl.ds(c0, _C_TILE), :] = (o * g_ref[pl.ds(c0, _C_TILE), :].astype(F32)).astype(BF16)
        return carry

    lax.fori_loop(0, ntile, finish, 0)


def _hgrn(layer, cq, cv, lff, lfb, cvt, cg, ones_c, gain, bsz, seq):
    tokens = bsz * seq
    nchunk = seq // C_CHUNK
    inter = pl.BlockSpec((1, C_CHUNK, nchunk, C_WIDTH), lambda b: (b, 0, 0, 0))
    return pl.pallas_call(
        functools.partial(_hgrn_kernel, seq),
        grid=(bsz,),
        in_specs=[
            inter, inter, inter, inter,
            pl.BlockSpec((C_WIDTH, seq), lambda b: (0, b)),
            pl.BlockSpec((seq, C_WIDTH), lambda b: (b, 0)),
            pl.BlockSpec((C_WIDTH, C_WIDTH), lambda b: (0, 0)),
            pl.BlockSpec((1, C_WIDTH), lambda b: (0, 0)),
        ],
        out_specs=pl.BlockSpec((seq, C_WIDTH), lambda b: (b, 0)),
        out_shape=jax.ShapeDtypeStruct((tokens, C_WIDTH), BF16),
        scratch_shapes=[
            pltpu.VMEM((C_CHUNK, nchunk, C_WIDTH), F32),
            pltpu.VMEM((C_CHUNK, nchunk, C_WIDTH), F32),
            pltpu.VMEM((nchunk, C_WIDTH), F32),
            pltpu.VMEM((C_WIDTH // LANES, seq, LANES), F32),
            pltpu.VMEM((C_WIDTH // LANES, seq, LANES), F32),
            pltpu.VMEM((_C_NPAIR * _C_BLOCK, C_WIDTH), BF16),
            pltpu.VMEM((C_CHUNK, nchunk, C_WIDTH), F32),
            pltpu.VMEM((C_WIDTH, C_WIDTH), F32),
            pltpu.VMEM((2, (_C_TILE // C_CHUNK) * C_WIDTH, C_WIDTH), F32),
            pltpu.VMEM((2, C_WIDTH, seq), F32),
            pltpu.VMEM((C_WIDTH // LANES, seq, LANES), F32),
        ],
        compiler_params=pltpu.CompilerParams(
            dimension_semantics=("arbitrary",), vmem_limit_bytes=VMEM_LIMIT_BYTES),
        name=f"hgrn{layer}",
    )(cq, cv, lff, lfb, cvt, cg, ones_c, gain)


_F_BLOCK = 256
_F_ROWS = 512


def _gelu(x):
    return 0.5 * x * (1.0 + lax.erf(x * (2.0 ** -0.5)))


def _ffn_kernel(seq, nsteps, x_hbm, oa_hbm, ob_hbm, oc_hbm, wo_ref, fn_ref,
                wgf_ref, wvf_ref, wg0_ref, wv0_ref, wg1_ref, wv1_ref,
                cw0_ref, cb0_ref, wd0_ref, cw1_ref, cb1_ref, wd1_ref, cwl_ref, cbl_ref, wdl_ref,
                o_ref, x_ref, oa_ref, ob_ref, oc_ref, in_sem, h_scr, g_scr, v_scr):
    b = pl.program_id(0)
    i = pl.program_id(1)
    rb = min(_F_ROWS, seq)
    halo = 8

    def input_copies(batch):
        rows = pl.ds(pl.multiple_of(batch * seq, seq), seq)
        pairs = ((x_hbm, x_ref), (oa_hbm, oa_ref), (ob_hbm, ob_ref), (oc_hbm, oc_ref))
        return [pltpu.make_async_copy(src.at[rows, :], dst, in_sem.at[n]) for n, (src, dst) in enumerate(pairs)]

    @pl.when(jnp.logical_and(b == 0, i == 0))
    def _():
        for cp in input_copies(b):
            cp.start()

    @pl.when(i == 0)
    def _():
        for cp in input_copies(b):
            cp.wait()

    def up(slot, r, wg_ref, wv_ref):
        rows = slice(r * rb, (r + 1) * rb)
        h = h_scr[rows, :]
        g_scr[slot, rows, :] = _dot(h, wg_ref[...])
        v_scr[slot, rows, :] = _dot(h, wv_ref[...])

    def consume(slot, r, cw_ref, cb_ref, wd_ref):
        lo = r * rb
        a, b = max(lo - halo, 0), min(lo + rb + halo, seq)
        n, off = b - a, lo - a
        g = g_scr[slot, a:b, :]
        row = lax.broadcasted_iota(jnp.int32, g.shape, 0) + a
        prev = jnp.where(row == 0, 0.0, pltpu.roll(g, 1, 0))[off:off + rb]
        nxt = jnp.where(row == seq - 1, 0.0, pltpu.roll(g, n - 1, 0))[off:off + rb]
        cw = cw_ref[...]
        conv = prev * cw[0:1, :] + g[off:off + rb] * cw[1:2, :] + nxt * cw[2:3, :] + cb_ref[...]
        act = (_gelu(conv) * v_scr[slot, lo:lo + rb, :]).astype(BF16)
        o_ref[lo:lo + rb, :] += _dot(act, wd_ref[...])

    nrb = seq // rb

    @pl.when(i == 0)
    def _():
        for r in range(nrb):
            rows = slice(r * rb, (r + 1) * rb)
            mix = jnp.concatenate([oa_ref[rows, :], ob_ref[rows, :], oc_ref[rows, :]], axis=1)
            xm = x_ref[rows, :] + _dot(mix, wo_ref[...])
            o_ref[rows, :] = xm
            ms = jnp.mean(xm * xm, axis=-1, keepdims=True)
            h_scr[rows, :] = (xm * lax.rsqrt(ms + EPS) * fn_ref[...]).astype(BF16)
            up(0, r, wgf_ref, wvf_ref)

    @pl.when(jnp.logical_and(i == 1, b + 1 < pl.num_programs(0)))
    def _():
        for cp in input_copies(b + 1):
            cp.start()

    for r in range(nrb):
        consume(0, r, cw0_ref, cb0_ref, wd0_ref)
        up(1, r, wg0_ref, wv0_ref)
    for r in range(nrb):
        consume(1, r, cw1_ref, cb1_ref, wd1_ref)
        up(0, r, wg1_ref, wv1_ref)

    @pl.when(i == nsteps - 1)
    def _():
        for r in range(nrb):
            consume(0, r, cwl_ref, cbl_ref, wdl_ref)


def _ffn(layer, xf, oa, ob, oc, wo, fn, wup, cw, cb, wd, bsz, seq):
    tokens = bsz * seq
    nf = D_FF // _F_BLOCK
    assert D_FF % _F_BLOCK == 0 and nf % 2 == 1
    nsteps = (nf - 1) // 2
    assert nsteps >= 2
    single = pl.Buffered(1)
    up_spec = lambda blk: [pl.BlockSpec((D_MODEL, _F_BLOCK), lambda b, i: (0, blk(i))),
                           pl.BlockSpec((D_MODEL, _F_BLOCK), lambda b, i: (0, blk(i) + nf))]
    down_spec = lambda blk: [pl.BlockSpec((3, _F_BLOCK), lambda b, i: (0, blk(i))),
                             pl.BlockSpec((1, _F_BLOCK), lambda b, i: (0, blk(i))),
                             pl.BlockSpec((_F_BLOCK, D_MODEL), lambda b, i: (blk(i), 0))]
    return pl.pallas_call(
        functools.partial(_ffn_kernel, seq, nsteps),
        grid=(bsz, nsteps),
        in_specs=[
            pl.BlockSpec(memory_space=pl.ANY),
            pl.BlockSpec(memory_space=pl.ANY),
            pl.BlockSpec(memory_space=pl.ANY),
            pl.BlockSpec(memory_space=pl.ANY),
            pl.BlockSpec((D_MIX, D_MODEL), lambda b, i: (0, 0), pipeline_mode=single),
            pl.BlockSpec((1, D_MODEL), lambda b, i: (0, 0)),
            *up_spec(lambda i: 0), *up_spec(lambda i: 2 * i + 1), *up_spec(lambda i: 2 * i + 2),
            *down_spec(lambda i: 2 * i), *down_spec(lambda i: 2 * i + 1), *down_spec(lambda i: nf - 1),
        ],
        out_specs=pl.BlockSpec((seq, D_MODEL), lambda b, i: (b, 0)),
        out_shape=jax.ShapeDtypeStruct((tokens, D_MODEL), F32),
        scratch_shapes=[
            pltpu.VMEM((seq, D_MODEL), F32),
            pltpu.VMEM((seq, A_WIDTH), BF16),
            pltpu.VMEM((seq, B_WIDTH), BF16),
            pltpu.VMEM((seq, C_WIDTH), BF16),
            pltpu.SemaphoreType.DMA((4,)),
            pltpu.VMEM((seq, D_MODEL), BF16),
            pltpu.VMEM((2, seq, _F_BLOCK), F32),
            pltpu.VMEM((2, seq, _F_BLOCK), F32),
        ],
        compiler_params=pltpu.CompilerParams(
            dimension_semantics=("arbitrary", "arbitrary"), vmem_limit_bytes=VMEM_LIMIT_BYTES),
        name=f"ffn{layer}",
    )(xf, oa, ob, oc, wo, fn, wup, wup, wup, wup, wup, wup, cw, cb, wd, cw, cb, wd, cw, cb, wd)


def _rope_tables(seq):
    rows = seq // GRID_W
    row = jnp.repeat(jnp.arange(rows), GRID_W)
    col = jnp.tile(jnp.arange(GRID_W), rows)
    inv_freq = ROPE_THETA ** (-jnp.arange(0, A_AXIS_DIM, 2, dtype=F32) / A_AXIS_DIM)
    ang = jnp.stack([row, col], axis=-1).astype(F32)[..., None] * inv_freq
    cos, sin = jnp.cos(ang), jnp.sin(ang)
    cos_h = jnp.concatenate([cos[:, 0], cos[:, 0], cos[:, 1], cos[:, 1]], axis=-1)
    sin_h = jnp.concatenate([-sin[:, 0], sin[:, 0], -sin[:, 1], sin[:, 1]], axis=-1)
    return jnp.tile(cos_h, (1, A_HEADS)), jnp.tile(sin_h, (1, A_HEADS))


def _partner_index(width):
    half = A_AXIS_DIM // 2
    i = np.arange(width)
    return np.where((i % A_AXIS_DIM) < half, i + half, i - half)


def _block_ones(width, block):
    i = np.arange(width) // block
    return jnp.asarray(i[:, None] == i[None, :], BF16)


def kernel(x, attn_norm, w_in, a_q_norm, a_k_norm, b_q_norm, b_k_norm, b_lambda, b_sub_norm,
           c_lb_logits, c_out_norm, w_out, ffn_norm, w_up, conv_w, conv_b, w_down):
    bsz, seq, _ = x.shape
    tokens = bsz * seq
    assert seq % 256 == 0 and seq % GRID_W == 0

    cos, sin = _rope_tables(seq)
    ones_a = _block_ones(A_WIDTH, A_HEAD_DIM)
    ones_b = _block_ones(B_QK_WIDTH, B_HEAD_DIM)
    ones_c = _block_ones(C_WIDTH, C_VAL_DIM)
    pq = _partner_index(A_WIDTH)
    pk = _partner_index(A_KV_WIDTH)

    dist = _alibi_distance_table(seq)

    xf = x.reshape(tokens, D_MODEL)
    for l in range(DEPTH):
        w = w_in[l]
        wq, wk, wv = (w[:, _OFF[0]:_OFF[1]], w[:, _OFF[1]:_OFF[2]], w[:, _OFF[2]:_OFF[3]])
        wa = jnp.concatenate([wq, wq[:, pq], wk, wk[:, pk], wv], axis=1).astype(BF16)
        wb = w[:, _OFF[3]:_OFF[6]].astype(BF16)
        wc = w[:, _OFF[6]:_OFF[11]].astype(BF16)
        gq = jnp.tile(a_q_norm[l], A_HEADS)
        gk = jnp.pad(jnp.tile(a_k_norm[l], A_KV_HEADS), (0, A_WIDTH - A_KV_WIDTH))
        gkp = jnp.pad(jnp.tile(a_k_norm[l], A_KV_HEADS)[pk], (0, A_WIDTH - A_KV_WIDTH))
        ga = jnp.stack([gq, gq[pq], gk, gkp])
        gb = jnp.stack([jnp.tile(b_q_norm[l], 2 * B_HEADS), jnp.tile(b_k_norm[l], 2 * B_HEADS)])

        (aq, akt, av, bq, bkt, bv, cq, cv, lff, lfb, cg, cvt) = _proj(
            l, xf, attn_norm[l][None, :], wa, wb, wc, ga, gb, c_lb_logits, cos, sin, ones_a, ones_b,
            bsz, seq)
        oa = _attn_a(l, aq, akt, av, jnp.stack([a_q_norm[l], a_k_norm[l]]), bsz, seq)
        ob = _attn_b(l, bq, bkt, bv, dist, b_lambda[l], b_sub_norm[l][None, :],
                     jnp.stack([b_q_norm[l], b_k_norm[l]]), bsz, seq)
        oc = _hgrn(l, cq, cv, lff, lfb, cvt, cg, ones_c, c_out_norm[l][None, :], bsz, seq)
        xf = _ffn(l, xf, oa, ob, oc, w_out[l].astype(BF16), ffn_norm[l][None, :],
                  w_up[l].astype(BF16), conv_w[l], conv_b[l][None, :], w_down[l].astype(BF16), bsz, seq)
    return xf.reshape(bsz, seq, D_MODEL)
```

```python
import functools
import math

import jax
import jax.numpy as jnp
import numpy as np
from jax import lax
from jax.experimental import pallas as pl
from jax.experimental.pallas import tpu as pltpu

D_MODEL = 1024
DEPTH = 4
GRID_W = 64
EPS = 1e-6
MIN_FORGET = 1e-20
LOG2E = 1.4426950408889634

A_HEADS, A_KV_HEADS, A_HEAD_DIM = 6, 2, 64
A_GROUP = A_HEADS // A_KV_HEADS
A_AXIS_DIM = A_HEAD_DIM // 2
ROPE_THETA = 10000.0
A_WIDTH = A_HEADS * A_HEAD_DIM
A_KV_WIDTH = A_KV_HEADS * A_HEAD_DIM

B_HEADS, B_HEAD_DIM = 4, 48
B_V_DIM = 2 * B_HEAD_DIM
B_QK_WIDTH = B_HEADS * 2 * B_HEAD_DIM
B_WIDTH = B_HEADS * B_V_DIM

C_HEADS, C_KEY_DIM, C_VAL_DIM = 4, 64, 64
C_WIDTH = C_HEADS * C_KEY_DIM
C_CHUNK = 16

D_MIX = A_WIDTH + B_WIDTH + C_WIDTH
D_FF = 2816

VMEM_LIMIT_BYTES = 56 * 1024 * 1024

_OFF = np.cumsum([0, A_WIDTH, A_KV_WIDTH, A_KV_WIDTH, B_QK_WIDTH, B_QK_WIDTH, B_WIDTH,
                  C_WIDTH, C_WIDTH, C_WIDTH, C_WIDTH, C_WIDTH])

BF16 = jnp.bfloat16
F32 = jnp.float32
LANES = 128


def _dot(a, b):
    return jnp.dot(a, b, preferred_element_type=F32)


def _dot_nt(a, b):
    return lax.dot_general(a, b, (((1,), (1,)), ((), ())), preferred_element_type=F32)


def _sigmoid(z):
    return 1.0 / (1.0 + jnp.exp(-z))


def _proj_kernel(layer, tm,
                 x_ref, an_ref, wa_ref, wb_ref, wc_ref, ga_ref, gb_ref, lbl_ref,
                 cos_ref, sin_ref, ones_a_ref, ones_b_ref,
                 aq_ref, akt_ref, av_ref, bq_ref, bkt_ref, bv_ref,
                 cq_ref, cv_ref, lff_ref, lfb_ref, cg_ref, cvt_ref,
                 scr_ref):
    x = x_ref[...]
    ms = jnp.mean(x * x, axis=-1, keepdims=True)
    h = (x * lax.rsqrt(ms + EPS) * an_ref[...]).astype(BF16)

    cos = cos_ref[...]
    sin = sin_ref[...]
    nchunk = tm // C_CHUNK

    def norm_rope(y, yp, width, g, gp, scale):
        ones = ones_a_ref[0:width, 0:width]
        ss = _dot((y * y).astype(BF16), ones)
        r = lax.rsqrt(ss * (1.0 / A_HEAD_DIM) + EPS)
        out = (y * g * cos[:, 0:width] + yp * gp * sin[:, 0:width]) * r
        return out * scale if scale != 1.0 else out

    def norm_b(y, g, scale):
        ss = _dot((y * y).astype(BF16), ones_b_ref[...])
        r = lax.rsqrt(ss * (1.0 / B_HEAD_DIM) + EPS)
        out = y * g * r
        return out * scale if scale != 1.0 else out

    def with_ones_column(v):
        tail = (lax.broadcasted_iota(jnp.int32, (tm, LANES - v.shape[1]), 1) == 0).astype(F32)
        return jnp.concatenate([v, tail], axis=1).astype(BF16)

    def interleave(slot, val, out_ref):
        for c in range(C_WIDTH // LANES):
            scr_ref[slot, c] = val[:, c * LANES:(c + 1) * LANES]
        for t in range(C_CHUNK):
            for c in range(C_WIDTH // LANES):
                out_ref[0, t, :, c * LANES:(c + 1) * LANES] = scr_ref[slot, c, pl.ds(t, nchunk, stride=C_CHUNK), :]

    def forget_lower_bound():
        logits = lbl_ref[...]
        mx = jnp.max(logits, axis=0, keepdims=True)
        e = jnp.exp(logits - mx)
        p = e / jnp.sum(e, axis=0, keepdims=True)
        cum = p[0:1, :]
        for i in range(1, layer + 1):
            cum = cum + p[i:i + 1, :]
        return jnp.maximum(cum - p[0:1, :], 0.0)

    def epi_aq(y):
        q = norm_rope(y[:, 0:A_WIDTH], y[:, A_WIDTH:2 * A_WIDTH], A_WIDTH,
                      ga_ref[0:1, 0:A_WIDTH], ga_ref[1:2, 0:A_WIDTH], A_HEAD_DIM ** -0.5 * LOG2E)
        aq_ref[...] = q.astype(BF16)

    def epi_ak(y):
        k = norm_rope(y[:, 0:A_KV_WIDTH], y[:, A_KV_WIDTH:2 * A_KV_WIDTH], A_KV_WIDTH,
                      ga_ref[2:3, 0:A_KV_WIDTH], ga_ref[3:4, 0:A_KV_WIDTH], 1.0)
        akt_ref[...] = k.T.astype(BF16)

    def epi_av(y):
        for j in range(A_KV_HEADS):
            av_ref[j] = with_ones_column(y[:, j * A_HEAD_DIM:(j + 1) * A_HEAD_DIM])

    def epi_bq(y):
        bq_ref[...] = norm_b(y, gb_ref[0:1, :], B_HEAD_DIM ** -0.5 * LOG2E).astype(BF16)

    def epi_bk(y):
        bkt_ref[...] = norm_b(y, gb_ref[1:2, :], 1.0).T.astype(BF16)

    def epi_bv(y):
        for j in range(B_HEADS):
            bv_ref[j] = with_ones_column(y[:, j * B_V_DIM:(j + 1) * B_V_DIM])

    def epi_cq(y):
        interleave(0, y * _sigmoid(y), cq_ref)

    def epi_forget(slot, lf_ref):
        def epi(z):
            lb = forget_lower_bound()
            f = jnp.maximum(lb + (1.0 - lb) * _sigmoid(z), MIN_FORGET)
            interleave(slot, jnp.log(f), lf_ref)
        return epi

    def epi_cv(y):
        interleave(3, y, cv_ref)
        cvt_ref[...] = y.T.astype(BF16)

    def epi_cg(y):
        cg_ref[...] = (y * _sigmoid(y)).astype(BF16)

    oa_k = 2 * A_WIDTH
    oa_v = oa_k + 2 * A_KV_WIDTH
    groups = [
        (wa_ref, 0, 2 * A_WIDTH, epi_aq),
        (wa_ref, oa_k, 2 * A_KV_WIDTH, epi_ak),
        (wa_ref, oa_v, A_KV_WIDTH, epi_av),
        (wb_ref, 0, B_QK_WIDTH, epi_bq),
        (wb_ref, B_QK_WIDTH, B_QK_WIDTH, epi_bk),
        (wb_ref, 2 * B_QK_WIDTH, B_WIDTH, epi_bv),
        (wc_ref, 0, C_WIDTH, epi_cq),
        (wc_ref, C_WIDTH, C_WIDTH, epi_forget(1, lff_ref)),
        (wc_ref, 2 * C_WIDTH, C_WIDTH, epi_forget(2, lfb_ref)),
        (wc_ref, 3 * C_WIDTH, C_WIDTH, epi_cv),
        (wc_ref, 4 * C_WIDTH, C_WIDTH, epi_cg),
    ]
    pending = None
    for w_ref, lo, width, epi in groups:
        y = _dot(h, w_ref[:, lo:lo + width])
        if pending is not None:
            pending[0](pending[1])
        pending = (epi, y)
    pending[0](pending[1])


def _proj(layer, xf, an, wa, wb, wc, ga, gb, lbl, cos, sin, ones_a, ones_b, bsz, seq):
    tokens = bsz * seq
    tm = min(512, seq)
    nt = seq // tm
    nchunk = tm // C_CHUNK
    const = lambda shape: pl.BlockSpec(shape, lambda i: (0,) * len(shape))
    tok = lambda w: pl.BlockSpec((tm, w), lambda i: (i, 0))
    tr = lambda w: pl.BlockSpec((w, tm), lambda i: (0, i))
    head = lambda nh, w: pl.BlockSpec((nh, tm, w), lambda i: (0, i, 0))
    inter = pl.BlockSpec((1, C_CHUNK, nchunk, C_WIDTH), lambda i: (i // nt, 0, i % nt, 0))
    inter_shape = jax.ShapeDtypeStruct((bsz, C_CHUNK, seq // C_CHUNK, C_WIDTH), F32)
    out_shape = (
        jax.ShapeDtypeStruct((tokens, A_WIDTH), BF16),
        jax.ShapeDtypeStruct((A_KV_WIDTH, tokens), BF16),
        jax.ShapeDtypeStruct((A_KV_HEADS, tokens, LANES), BF16),
        jax.ShapeDtypeStruct((tokens, B_QK_WIDTH), BF16),
        jax.ShapeDtypeStruct((B_QK_WIDTH, tokens), BF16),
        jax.ShapeDtypeStruct((B_HEADS, tokens, LANES), BF16),
        inter_shape, inter_shape, inter_shape, inter_shape,
        jax.ShapeDtypeStruct((tokens, C_WIDTH), BF16),
        jax.ShapeDtypeStruct((C_WIDTH, tokens), BF16),
    )
    out_specs = (
        tok(A_WIDTH), tr(A_KV_WIDTH), head(A_KV_HEADS, LANES),
        tok(B_QK_WIDTH), tr(B_QK_WIDTH), head(B_HEADS, LANES),
        inter, inter, inter, inter,
        tok(C_WIDTH), tr(C_WIDTH),
    )
    in_specs = [
        tok(D_MODEL), const((1, D_MODEL)),
        const(wa.shape), const(wb.shape), const(wc.shape),
        const(ga.shape), const(gb.shape), const(lbl.shape),
        pl.BlockSpec((tm, A_WIDTH), lambda i: (i % nt, 0)),
        pl.BlockSpec((tm, A_WIDTH), lambda i: (i % nt, 0)),
        const(ones_a.shape), const(ones_b.shape),
    ]
    return pl.pallas_call(
        functools.partial(_proj_kernel, layer, tm),
        grid=(tokens // tm,),
        in_specs=in_specs,
        out_specs=out_specs,
        out_shape=out_shape,
        scratch_shapes=[pltpu.VMEM((4, C_WIDTH // LANES, tm, LANES), F32)],
        compiler_params=pltpu.CompilerParams(
            dimension_semantics=("arbitrary",), vmem_limit_bytes=VMEM_LIMIT_BYTES),
        name=f"proj{layer}",
    )(xf, an, wa, wb, wc, ga, gb, lbl, cos, sin, ones_a, ones_b)


_L_MIN = 2.0 ** -100
_BOUND_SLACK = 1.02


def _score_bound(g_ref, dim):
    g = jnp.abs(g_ref[...])
    gq = jnp.max(g[0:1, :], axis=-1, keepdims=True)
    gk = jnp.max(g[1:2, :], axis=-1, keepdims=True)
    return gq * gk * (dim ** 0.5 * LOG2E * _BOUND_SLACK)


def _attn_a_kernel(tq, tk, q_ref, kt_ref, v_ref, g_ref, o_ref):
    seq = kt_ref.shape[1]
    nk = seq // tk
    bound = _score_bound(g_ref, A_HEAD_DIM)

    def group_q(g):
        heads = [A_GROUP * g + j for j in range(A_GROUP)]
        return heads, jnp.concatenate(
            [q_ref[:, h * A_HEAD_DIM:(h + 1) * A_HEAD_DIM] for h in heads], axis=0)

    def scores(qg, g, c):
        return _dot(qg, kt_ref[g * A_HEAD_DIM:(g + 1) * A_HEAD_DIM, c * tk:(c + 1) * tk])

    def store(g, heads, o):
        for j, h in enumerate(heads):
            o_ref[:, h * A_HEAD_DIM:(h + 1) * A_HEAD_DIM] = o[j * tq:(j + 1) * tq, 0:A_HEAD_DIM].astype(BF16)

    lmin = None
    for g in range(A_KV_HEADS):
        heads, qg = group_q(g)
        acc = None
        for c in range(nk):
            p = jnp.exp2(scores(qg, g, c) - bound).astype(BF16)
            pv = _dot(p, v_ref[g, c * tk:(c + 1) * tk, :])
            acc = pv if acc is None else acc + pv
        l = acc[:, A_HEAD_DIM:A_HEAD_DIM + 1]
        store(g, heads, acc * (1.0 / l))
        lg = jnp.min(l, axis=0, keepdims=True)
        lmin = lg if lmin is None else jnp.minimum(lmin, lg)

    @pl.when(jnp.logical_not(lmin[0, 0] >= _L_MIN))
    def _():
        for g in range(A_KV_HEADS):
            heads, qg = group_q(g)
            m = acc = None
            for c in range(nk):
                s = scores(qg, g, c)
                mc = jnp.max(s, axis=-1, keepdims=True)
                vc = v_ref[g, c * tk:(c + 1) * tk, :]
                if c == 0:
                    m = mc
                    acc = _dot(jnp.exp2(s - m).astype(BF16), vc)
                else:
                    mn = jnp.maximum(m, mc)
                    acc = jnp.exp2(m - mn) * acc + _dot(jnp.exp2(s - mn).astype(BF16), vc)
                    m = mn
            store(g, heads, acc * (1.0 / acc[:, A_HEAD_DIM:A_HEAD_DIM + 1]))


def _attn_a(layer, aq, akt, av, gains, bsz, seq):
    tokens = bsz * seq
    tq = min(512, seq)
    tk = min(256, seq)
    nq = seq // tq
    return pl.pallas_call(
        functools.partial(_attn_a_kernel, tq, tk),
        grid=(bsz, nq),
        in_specs=[
            pl.BlockSpec((tq, A_WIDTH), lambda b, i: (b * nq + i, 0)),
            pl.BlockSpec((A_KV_WIDTH, seq), lambda b, i: (0, b)),
            pl.BlockSpec((A_KV_HEADS, seq, LANES), lambda b, i: (0, b, 0)),
            pl.BlockSpec(gains.shape, lambda b, i: (0, 0)),
        ],
        out_specs=pl.BlockSpec((tq, A_WIDTH), lambda b, i: (b * nq + i, 0)),
        out_shape=jax.ShapeDtypeStruct((tokens, A_WIDTH), BF16),
        compiler_params=pltpu.CompilerParams(
            dimension_semantics=("arbitrary", "arbitrary"), vmem_limit_bytes=VMEM_LIMIT_BYTES),
        name=f"attn_a{layer}",
    )(aq, akt, av, gains)


def _attn_b_kernel(layer, tq, tk, nq, seq, q_ref, kt_ref, v_ref, dist_ref, lam_ref, gain_ref, g_ref, o_ref):
    i = pl.program_id(1)
    lam_p = lam_ref[...]
    lam_init = 0.8 - 0.6 * math.exp(-0.3 * layer)
    lam = (jnp.exp(jnp.sum(lam_p[0:1] * lam_p[1:2], axis=-1, keepdims=True))
           - jnp.exp(jnp.sum(lam_p[2:3] * lam_p[3:4], axis=-1, keepdims=True)) + lam_init)
    off = pl.multiple_of((nq - 1 - i) * tq, tq)
    nk = seq // tk
    bound = _score_bound(g_ref, B_HEAD_DIM)

    def scores(h, c, kc):
        lo = (2 * h + c) * B_HEAD_DIM
        slope = 2.0 ** (-8.0 * (h + 1) / B_HEADS)
        bias = dist_ref[:, pl.ds(off + kc * tk, tk)] * slope
        return _dot(q_ref[:, lo:lo + B_HEAD_DIM], kt_ref[lo:lo + B_HEAD_DIM, kc * tk:(kc + 1) * tk]) + bias

    def finish(h, outs):
        o = outs[0] - lam * outs[1]
        ms = jnp.mean(o * o, axis=-1, keepdims=True)
        o = o * lax.rsqrt(ms + EPS) * gain_ref[:, h * B_V_DIM:(h + 1) * B_V_DIM] * (1.0 - lam_init)
        o_ref[:, h * B_V_DIM:(h + 1) * B_V_DIM] = o.astype(BF16)

    lmin = None
    for h in range(B_HEADS):
        outs = []
        for c in range(2):
            acc = None
            for kc in range(nk):
                p = jnp.exp2(scores(h, c, kc) - bound).astype(BF16)
                pv = _dot(p, v_ref[h, kc * tk:(kc + 1) * tk, :])
                acc = pv if acc is None else acc + pv
            l = acc[:, B_V_DIM:B_V_DIM + 1]
            outs.append(acc[:, 0:B_V_DIM] * (1.0 / l))
            lg = jnp.min(l, axis=0, keepdims=True)
            lmin = lg if lmin is None else jnp.minimum(lmin, lg)
        finish(h, outs)

    @pl.when(jnp.logical_not(lmin[0, 0] >= _L_MIN))
    def _():
        for h in range(B_HEADS):
            outs = []
            for c in range(2):
                m = acc = None
                for kc in range(nk):
                    s = scores(h, c, kc)
                    mc = jnp.max(s, axis=-1, keepdims=True)
                    vc = v_ref[h, kc * tk:(kc + 1) * tk, :]
                    if kc == 0:
                        m = mc
                        acc = _dot(jnp.exp2(s - m).astype(BF16), vc)
                    else:
                        mn = jnp.maximum(m, mc)
                        acc = jnp.exp2(m - mn) * acc + _dot(jnp.exp2(s - mn).astype(BF16), vc)
                        m = mn
                outs.append(acc[:, 0:B_V_DIM] * (1.0 / acc[:, B_V_DIM:B_V_DIM + 1]))
            finish(h, outs)


def _attn_b_tiles(seq):
    return min(256, seq), min(1024, seq)


def _alibi_distance_table(seq):
    tq, _ = _attn_b_tiles(seq)
    r = jnp.arange(tq, dtype=jnp.int32)[:, None]
    u = jnp.arange(2 * seq - tq, dtype=jnp.int32)[None, :]
    return (-LOG2E) * jnp.abs(r - u + (seq - tq)).astype(F32)


def _attn_b(layer, bq, bkt, bv, dist, lam, gain, qk_gains, bsz, seq):
    tokens = bsz * seq
    tq, tk = _attn_b_tiles(seq)
    nq = seq // tq
    return pl.pallas_call(
        functools.partial(_attn_b_kernel, layer, tq, tk, nq, seq),
        grid=(bsz, nq),
        in_specs=[
            pl.BlockSpec((tq, B_QK_WIDTH), lambda b, i: (b * nq + i, 0)),
            pl.BlockSpec((B_QK_WIDTH, seq), lambda b, i: (0, b)),
            pl.BlockSpec((B_HEADS, seq, LANES), lambda b, i: (0, b, 0)),
            pl.BlockSpec(dist.shape, lambda b, i: (0, 0), pipeline_mode=pl.Buffered(1)),
            pl.BlockSpec(lam.shape, lambda b, i: (0, 0)),
            pl.BlockSpec(gain.shape, lambda b, i: (0, 0)),
            pl.BlockSpec(qk_gains.shape, lambda b, i: (0, 0)),
        ],
        out_specs=pl.BlockSpec((tq, B_WIDTH), lambda b, i: (b * nq + i, 0)),
        out_shape=jax.ShapeDtypeStruct((tokens, B_WIDTH), BF16),
        compiler_params=pltpu.CompilerParams(
            dimension_semantics=("arbitrary", "arbitrary"), vmem_limit_bytes=VMEM_LIMIT_BYTES),
        name=f"attn_b{layer}",
    )(bq, bkt, bv, dist, lam, gain, qk_gains)


_C_BLOCK = 16
_C_TILE = 128
_C_NPAIR = C_CHUNK * (C_CHUNK + 1) // 2
_C_PAIR_GROUPS = ((0, 8), (8, 11), (11, 14), (14, 16))


def _hgrn_kernel(seq, q_ref, v_ref, lff_ref, lfb_ref, vt_ref, g_ref, r_ref, gain_ref,
                 o_ref,
                 b_scr, k_scr, dec_scr, qn_scr, kn_scr, tt_scr, oi_scr, st_scr, kv_scr, ot_scr, on_scr):
    nchunk = seq // C_CHUNK
    ntile = seq // _C_TILE
    cpt = _C_TILE // C_CHUNK

    def same_head():
        rows_h = lax.broadcasted_iota(jnp.int32, (C_WIDTH, C_WIDTH), 0) // C_VAL_DIM
        cols_h = lax.broadcasted_iota(jnp.int32, (C_WIDTH, C_WIDTH), 1) // C_KEY_DIM
        return rows_h == cols_h

    nhalf = C_WIDTH // LANES

    def scatter(dst_ref, n0, t, val):
        for c in range(nhalf):
            dst_ref[c, pl.ds(n0 * C_CHUNK + t, val.shape[0], stride=C_CHUNK), :] = val[:, c * LANES:(c + 1) * LANES]

    def token_rows(src_ref, c0):
        return jnp.concatenate([src_ref[c, pl.ds(c0, _C_TILE), :] for c in range(nhalf)], axis=1)

    groups = _C_PAIR_GROUPS
    offs = [sum(j + 1 for j in range(groups[g][0])) for g in range(len(groups))]

    for d, lf_ref in enumerate((lff_ref, lfb_ref)):
        pos = (lambda j: j) if d == 0 else (lambda j: C_CHUNK - 1 - j)
        order = list(range(cpt)) if d == 0 else list(range(cpt - 1, -1, -1))
        tile_at = (lambda s: s) if d == 0 else (lambda s: ntile - 1 - s)

        def prepare(cb, carry):
            n0 = pl.multiple_of(cb * _C_BLOCK, _C_BLOCK)
            rows = pl.ds(n0, _C_BLOCK)
            b = None
            for j in range(C_CHUNK):
                lfj = lf_ref[0, pos(j), rows, :]
                k_scr[pos(j), rows, :] = 1.0 - jnp.exp(lfj)
                b = lfj if b is None else b + lfj
                b_scr[j, rows, :] = b
                scatter(qn_scr, n0, pos(j), q_ref[0, pos(j), rows, :] * jnp.exp(b))
            dec_scr[rows, :] = jnp.exp(b)
            for j in range(C_CHUNK):
                scatter(kn_scr, n0, pos(j), k_scr[pos(j), rows, :] * jnp.exp(b - b_scr[j, rows, :]))
            return carry

        lax.fori_loop(0, nchunk // _C_BLOCK, prepare, 0)

        def intra(cb, carry):
            rows = pl.ds(pl.multiple_of(cb * _C_BLOCK, _C_BLOCK), _C_BLOCK)

            def build(g):
                pi = offs[g]
                for j in range(*groups[g]):
                    bj = b_scr[j, rows, :]
                    qj = q_ref[0, pos(j), rows, :]
                    for i in range(j + 1):
                        w = jnp.exp(bj - b_scr[i, rows, :]) * qj * k_scr[pos(i), rows, :]
                        tt_scr[pi * _C_BLOCK:(pi + 1) * _C_BLOCK, :] = w.astype(BF16)
                        pi += 1
                return _dot(tt_scr[offs[g] * _C_BLOCK:pi * _C_BLOCK, :], r_ref[...])

            def consume(g, y):
                pi = 0
                for j in range(*groups[g]):
                    acc = None
                    for i in range(j + 1):
                        term = y[pi * _C_BLOCK:(pi + 1) * _C_BLOCK, :] * v_ref[0, pos(i), rows, :]
                        acc = term if acc is None else acc + term
                        pi += 1
                    if d == 0:
                        oi_scr[pos(j), rows, :] = acc
                    else:
                        oi_scr[pos(j), rows, :] = oi_scr[pos(j), rows, :] + acc

            prev = None
            for g in range(len(groups)):
                y = build(g)
                if prev is not None:
                    consume(*prev)
                prev = (g, y)
            consume(*prev)
            return carry

        lax.fori_loop(0, nchunk // _C_BLOCK, intra, 0)

        def kv_products(slot, tile):
            c0 = pl.multiple_of(tile * _C_TILE, _C_TILE)
            kt = token_rows(kn_scr, c0).astype(BF16)
            vt = vt_ref[:, pl.ds(c0, _C_TILE)]
            lane_chunk = lax.broadcasted_iota(jnp.int32, (C_WIDTH, _C_TILE), 1) // C_CHUNK
            vexp = jnp.concatenate(
                [jnp.where(lane_chunk == n, vt, jnp.zeros_like(vt)) for n in order], axis=0)
            kv_scr[slot] = _dot(vexp, kt)

        def chain(slot, tile):
            c0 = pl.multiple_of(tile * _C_TILE, _C_TILE)
            qt = token_rows(qn_scr, c0).astype(BF16)
            row_chunk = lax.broadcasted_iota(jnp.int32, (_C_TILE, C_WIDTH), 0) // C_CHUNK
            diag = same_head()
            st = st_scr[...]
            acc = None
            for nn, n in enumerate(order):
                qm = jnp.where(row_chunk == n, qt, jnp.zeros_like(qt))
                pv = _dot_nt(st.astype(BF16), qm)
                acc = pv if acc is None else acc + pv
                drow = dec_scr[pl.ds(tile * cpt + n, 1), :]
                kvn = kv_scr[slot, nn * C_WIDTH:(nn + 1) * C_WIDTH, :]
                st = st * drow + jnp.where(diag, kvn, 0.0)
            st_scr[...] = st
            ot_scr[d, :, pl.ds(c0, _C_TILE)] = acc

        st_scr[...] = jnp.zeros_like(st_scr)
        kv_products(0, tile_at(0))

        def inter(it, carry):
            s0 = 2 * it
            kv_products(1, tile_at(s0 + 1))
            chain(0, tile_at(s0))
            kv_products(0, tile_at(jnp.minimum(s0 + 2, ntile - 1)))
            chain(1, tile_at(s0 + 1))
            return carry

        lax.fori_loop(0, ntile // 2, inter, 0)

    def reorder(cb, carry):
        n0 = pl.multiple_of(cb * _C_BLOCK, _C_BLOCK)
        for t in range(C_CHUNK):
            scatter(on_scr, n0, t, oi_scr[t, pl.ds(n0, _C_BLOCK), :])
        return carry

    lax.fori_loop(0, nchunk // _C_BLOCK, reorder, 0)

    def finish(ti, carry):
        c0 = pl.multiple_of(ti * _C_TILE, _C_TILE)
        o = token_rows(on_scr, c0) + (ot_scr[0, :, pl.ds(c0, _C_TILE)] + ot_scr[1, :, pl.ds(c0, _C_TILE)]).T
        ss = _dot((o * o).astype(BF16), r_ref[...])
        o = o * lax.rsqrt(ss * (1.0 / C_VAL_DIM) + EPS) * gain_ref[...]
        o_ref[pl.ds(c0, _C_TILE), :] = (o * g_ref[pl.ds(c0, _C_TILE), :].astype(F32)).astype(BF16)
        return carry

    lax.fori_loop(0, ntile, finish, 0)


def _hgrn(layer, cq, cv, lff, lfb, cvt, cg, ones_c, gain, bsz, seq):
    tokens = bsz * seq
    nchunk = seq // C_CHUNK
    inter = pl.BlockSpec((1, C_CHUNK, nchunk, C_WIDTH), lambda b: (b, 0, 0, 0))
    return pl.pallas_call(
        functools.partial(_hgrn_kernel, seq),
        grid=(bsz,),
        in_specs=[
            inter, inter, inter, inter,
            pl.BlockSpec((C_WIDTH, seq), lambda b: (0, b)),
            pl.BlockSpec((seq, C_WIDTH), lambda b: (b, 0)),
            pl.BlockSpec((C_WIDTH, C_WIDTH), lambda b: (0, 0)),
            pl.BlockSpec((1, C_WIDTH), lambda b: (0, 0)),
        ],
        out_specs=pl.BlockSpec((seq, C_WIDTH), lambda b: (b, 0)),
        out_shape=jax.ShapeDtypeStruct((tokens, C_WIDTH), BF16),
        scratch_shapes=[
            pltpu.VMEM((C_CHUNK, nchunk, C_WIDTH), F32),
            pltpu.VMEM((C_CHUNK, nchunk, C_WIDTH), F32),
            pltpu.VMEM((nchunk, C_WIDTH), F32),
            pltpu.VMEM((C_WIDTH // LANES, seq, LANES), F32),
            pltpu.VMEM((C_WIDTH // LANES, seq, LANES), F32),
            pltpu.VMEM((_C_NPAIR * _C_BLOCK, C_WIDTH), BF16),
            pltpu.VMEM((C_CHUNK, nchunk, C_WIDTH), F32),
            pltpu.VMEM((C_WIDTH, C_WIDTH), F32),
            pltpu.VMEM((2, (_C_TILE // C_CHUNK) * C_WIDTH, C_WIDTH), F32),
            pltpu.VMEM((2, C_WIDTH, seq), F32),
            pltpu.VMEM((C_WIDTH // LANES, seq, LANES), F32),
        ],
        compiler_params=pltpu.CompilerParams(
            dimension_semantics=("arbitrary",), vmem_limit_bytes=VMEM_LIMIT_BYTES),
        name=f"hgrn{layer}",
    )(cq, cv, lff, lfb, cvt, cg, ones_c, gain)


_F_BLOCK = 256
_F_ROWS = 256
_F_OUT_ROWS = 512


def _gelu(x):
    return 0.5 * x * (1.0 + lax.erf(x * (2.0 ** -0.5)))


def _ffn_kernel(seq, nsteps, x_hbm, oa_hbm, ob_hbm, oc_hbm, wo_ref, fn_ref,
                wgf_ref, wvf_ref, wg0_ref, wv0_ref, wg1_ref, wv1_ref,
                cw0_ref, cb0_ref, wd0_ref, cw1_ref, cb1_ref, wd1_ref, cwl_ref, cbl_ref, wdl_ref,
                o_ref, x_ref, oa_ref, ob_ref, oc_ref, in_sem, h_scr, g_scr, v_scr):
    b = pl.program_id(0)
    i = pl.program_id(1)
    rb = min(_F_ROWS, seq)
    halo = 8

    def input_copies(batch):
        rows = pl.ds(pl.multiple_of(batch * seq, seq), seq)
        pairs = ((x_hbm, x_ref), (oa_hbm, oa_ref), (ob_hbm, ob_ref), (oc_hbm, oc_ref))
        return [pltpu.make_async_copy(src.at[rows, :], dst, in_sem.at[n]) for n, (src, dst) in enumerate(pairs)]

    @pl.when(jnp.logical_and(b == 0, i == 0))
    def _():
        for cp in input_copies(b):
            cp.start()

    @pl.when(i == 0)
    def _():
        for cp in input_copies(b):
            cp.wait()

    def up(slot, r, wg_ref, wv_ref):
        rows = slice(r * rb, (r + 1) * rb)
        h = h_scr[rows, :]
        g_scr[slot, rows, :] = _dot(h, wg_ref[...])
        v_scr[slot, rows, :] = _dot(h, wv_ref[...])

    def consume(slot, r, cw_ref, cb_ref, wd_ref):
        lo = r * rb
        a, b = max(lo - halo, 0), min(lo + rb + halo, seq)
        n, off = b - a, lo - a
        g = g_scr[slot, a:b, :]
        row = lax.broadcasted_iota(jnp.int32, g.shape, 0) + a
        prev = jnp.where(row == 0, 0.0, pltpu.roll(g, 1, 0))[off:off + rb]
        nxt = jnp.where(row == seq - 1, 0.0, pltpu.roll(g, n - 1, 0))[off:off + rb]
        cw = cw_ref[...]
        conv = prev * cw[0:1, :] + g[off:off + rb] * cw[1:2, :] + nxt * cw[2:3, :] + cb_ref[...]
        act = (_gelu(conv) * v_scr[slot, lo:lo + rb, :]).astype(BF16)
        o_ref[lo:lo + rb, :] += _dot(act, wd_ref[...])

    nrb = seq // rb

    @pl.when(i == 0)
    def _():
        prb = max(rb, min(_F_OUT_ROWS, seq))
        for r in range(seq // prb):
            rows = slice(r * prb, (r + 1) * prb)
            mix = jnp.concatenate([oa_ref[rows, :], ob_ref[rows, :], oc_ref[rows, :]], axis=1)
            xm = x_ref[rows, :] + _dot(mix, wo_ref[...])
            o_ref[rows, :] = xm
            ms = jnp.mean(xm * xm, axis=-1, keepdims=True)
            h_scr[rows, :] = (xm * lax.rsqrt(ms + EPS) * fn_ref[...]).astype(BF16)
            for rr in range(r * (prb // rb), (r + 1) * (prb // rb)):
                up(0, rr, wgf_ref, wvf_ref)

    @pl.when(jnp.logical_and(i == 1, b + 1 < pl.num_programs(0)))
    def _():
        for cp in input_copies(b + 1):
            cp.start()

    for r in range(nrb):
        consume(0, r, cw0_ref, cb0_ref, wd0_ref)
        up(1, r, wg0_ref, wv0_ref)
    for r in range(nrb):
        consume(1, r, cw1_ref, cb1_ref, wd1_ref)
        up(0, r, wg1_ref, wv1_ref)

    @pl.when(i == nsteps - 1)
    def _():
        for r in range(nrb):
            consume(0, r, cwl_ref, cbl_ref, wdl_ref)


def _ffn(layer, xf, oa, ob, oc, wo, fn, wup, cw, cb, wd, bsz, seq):
    tokens = bsz * seq
    nf = D_FF // _F_BLOCK
    assert D_FF % _F_BLOCK == 0 and nf % 2 == 1
    nsteps = (nf - 1) // 2
    assert nsteps >= 2
    single = pl.Buffered(1)
    up_spec = lambda blk: [pl.BlockSpec((D_MODEL, _F_BLOCK), lambda b, i: (0, blk(i))),
                           pl.BlockSpec((D_MODEL, _F_BLOCK), lambda b, i: (0, blk(i) + nf))]
    down_spec = lambda blk: [pl.BlockSpec((3, _F_BLOCK), lambda b, i: (0, blk(i))),
                             pl.BlockSpec((1, _F_BLOCK), lambda b, i: (0, blk(i))),
                             pl.BlockSpec((_F_BLOCK, D_MODEL), lambda b, i: (blk(i), 0))]
    return pl.pallas_call(
        functools.partial(_ffn_kernel, seq, nsteps),
        grid=(bsz, nsteps),
        in_specs=[
            pl.BlockSpec(memory_space=pl.ANY),
            pl.BlockSpec(memory_space=pl.ANY),
            pl.BlockSpec(memory_space=pl.ANY),
            pl.BlockSpec(memory_space=pl.ANY),
            pl.BlockSpec((D_MIX, D_MODEL), lambda b, i: (0, 0), pipeline_mode=single),
            pl.BlockSpec((1, D_MODEL), lambda b, i: (0, 0)),
            *up_spec(lambda i: 0), *up_spec(lambda i: 2 * i + 1), *up_spec(lambda i: 2 * i + 2),
            *down_spec(lambda i: 2 * i), *down_spec(lambda i: 2 * i + 1), *down_spec(lambda i: nf - 1),
        ],
        out_specs=pl.BlockSpec((seq, D_MODEL), lambda b, i: (b, 0)),
        out_shape=jax.ShapeDtypeStruct((tokens, D_MODEL), F32),
        scratch_shapes=[
            pltpu.VMEM((seq, D_MODEL), F32),
            pltpu.VMEM((seq, A_WIDTH), BF16),
            pltpu.VMEM((seq, B_WIDTH), BF16),
            pltpu.VMEM((seq, C_WIDTH), BF16),
            pltpu.SemaphoreType.DMA((4,)),
            pltpu.VMEM((seq, D_MODEL), BF16),
            pltpu.VMEM((2, seq, _F_BLOCK), F32),
            pltpu.VMEM((2, seq, _F_BLOCK), F32),
        ],
        compiler_params=pltpu.CompilerParams(
            dimension_semantics=("arbitrary", "arbitrary"), vmem_limit_bytes=VMEM_LIMIT_BYTES),
        name=f"ffn{layer}",
    )(xf, oa, ob, oc, wo, fn, wup, wup, wup, wup, wup, wup, cw, cb, wd, cw, cb, wd, cw, cb, wd)


def _rope_tables(seq):
    rows = seq // GRID_W
    row = jnp.repeat(jnp.arange(rows), GRID_W)
    col = jnp.tile(jnp.arange(GRID_W), rows)
    inv_freq = ROPE_THETA ** (-jnp.arange(0, A_AXIS_DIM, 2, dtype=F32) / A_AXIS_DIM)
    ang = jnp.stack([row, col], axis=-1).astype(F32)[..., None] * inv_freq
    cos, sin = jnp.cos(ang), jnp.sin(ang)
    cos_h = jnp.concatenate([cos[:, 0], cos[:, 0], cos[:, 1], cos[:, 1]], axis=-1)
    sin_h = jnp.concatenate([-sin[:, 0], sin[:, 0], -sin[:, 1], sin[:, 1]], axis=-1)
    return jnp.tile(cos_h, (1, A_HEADS)), jnp.tile(sin_h, (1, A_HEADS))


def _partner_index(width):
    half = A_AXIS_DIM // 2
    i = np.arange(width)
    return np.where((i % A_AXIS_DIM) < half, i + half, i - half)


def _block_ones(width, block):
    i = np.arange(width) // block
    return jnp.asarray(i[:, None] == i[None, :], BF16)


def kernel(x, attn_norm, w_in, a_q_norm, a_k_norm, b_q_norm, b_k_norm, b_lambda, b_sub_norm,
           c_lb_logits, c_out_norm, w_out, ffn_norm, w_up, conv_w, conv_b, w_down):
    bsz, seq, _ = x.shape
    tokens = bsz * seq
    assert seq % 256 == 0 and seq % GRID_W == 0

    cos, sin = _rope_tables(seq)
    ones_a = _block_ones(A_WIDTH, A_HEAD_DIM)
    ones_b = _block_ones(B_QK_WIDTH, B_HEAD_DIM)
    ones_c = _block_ones(C_WIDTH, C_VAL_DIM)
    pq = _partner_index(A_WIDTH)
    pk = _partner_index(A_KV_WIDTH)

    dist = _alibi_distance_table(seq)

    xf = x.reshape(tokens, D_MODEL)
    for l in range(DEPTH):
        w = w_in[l]
        wq, wk, wv = (w[:, _OFF[0]:_OFF[1]], w[:, _OFF[1]:_OFF[2]], w[:, _OFF[2]:_OFF[3]])
        wa = jnp.concatenate([wq, wq[:, pq], wk, wk[:, pk], wv], axis=1).astype(BF16)
        wb = w[:, _OFF[3]:_OFF[6]].astype(BF16)
        wc = w[:, _OFF[6]:_OFF[11]].astype(BF16)
        gq = jnp.tile(a_q_norm[l], A_HEADS)
        gk = jnp.pad(jnp.tile(a_k_norm[l], A_KV_HEADS), (0, A_WIDTH - A_KV_WIDTH))
        gkp = jnp.pad(jnp.tile(a_k_norm[l], A_KV_HEADS)[pk], (0, A_WIDTH - A_KV_WIDTH))
        ga = jnp.stack([gq, gq[pq], gk, gkp])
        gb = jnp.stack([jnp.tile(b_q_norm[l], 2 * B_HEADS), jnp.tile(b_k_norm[l], 2 * B_HEADS)])

        (aq, akt, av, bq, bkt, bv, cq, cv, lff, lfb, cg, cvt) = _proj(
            l, xf, attn_norm[l][None, :], wa, wb, wc, ga, gb, c_lb_logits, cos, sin, ones_a, ones_b,
            bsz, seq)
        oa = _attn_a(l, aq, akt, av, jnp.stack([a_q_norm[l], a_k_norm[l]]), bsz, seq)
        ob = _attn_b(l, bq, bkt, bv, dist, b_lambda[l], b_sub_norm[l][None, :],
                     jnp.stack([b_q_norm[l], b_k_norm[l]]), bsz, seq)
        oc = _hgrn(l, cq, cv, lff, lfb, cvt, cg, ones_c, c_out_norm[l][None, :], bsz, seq)
        xf = _ffn(l, xf, oa, ob, oc, w_out[l].astype(BF16), ffn_norm[l][None, :],
                  w_up[l].astype(BF16), conv_w[l], conv_b[l][None, :], w_down[l].astype(BF16), bsz, seq)
    return xf.reshape(bsz, seq, D_MODEL)
```

```python
import functools
import math

import jax
import jax.numpy as jnp
import numpy as np
from jax import lax
from jax.experimental import pallas as pl
from jax.experimental.pallas import tpu as pltpu

D_MODEL = 1024
DEPTH = 4
GRID_W = 64
EPS = 1e-6
MIN_FORGET = 1e-20
LOG2E = 1.4426950408889634

A_HEADS, A_KV_HEADS, A_HEAD_DIM = 6, 2, 64
A_GROUP = A_HEADS // A_KV_HEADS
A_AXIS_DIM = A_HEAD_DIM // 2
ROPE_THETA = 10000.0
A_WIDTH = A_HEADS * A_HEAD_DIM
A_KV_WIDTH = A_KV_HEADS * A_HEAD_DIM

B_HEADS, B_HEAD_DIM = 4, 48
B_V_DIM = 2 * B_HEAD_DIM
B_QK_WIDTH = B_HEADS * 2 * B_HEAD_DIM
B_WIDTH = B_HEADS * B_V_DIM

C_HEADS, C_KEY_DIM, C_VAL_DIM = 4, 64, 64
C_WIDTH = C_HEADS * C_KEY_DIM
C_CHUNK = 16

D_MIX = A_WIDTH + B_WIDTH + C_WIDTH
D_FF = 2816

VMEM_LIMIT_BYTES = 56 * 1024 * 1024

_OFF = np.cumsum([0, A_WIDTH, A_KV_WIDTH, A_KV_WIDTH, B_QK_WIDTH, B_QK_WIDTH, B_WIDTH,
                  C_WIDTH, C_WIDTH, C_WIDTH, C_WIDTH, C_WIDTH])

BF16 = jnp.bfloat16
F32 = jnp.float32
LANES = 128


def _dot(a, b):
    return jnp.dot(a, b, preferred_element_type=F32)


def _dot_nt(a, b):
    return lax.dot_general(a, b, (((1,), (1,)), ((), ())), preferred_element_type=F32)


def _sigmoid(z):
    return 1.0 / (1.0 + jnp.exp(-z))


def _proj_kernel(layer, tm,
                 x_ref, an_ref, wa_ref, wb_ref, wc_ref, ga_ref, gb_ref, lbl_ref,
                 cos_ref, sin_ref, ones_a_ref, ones_b_ref,
                 aq_ref, akt_ref, av_ref, bq_ref, bkt_ref, bv_ref,
                 cq_ref, cv_ref, lff_ref, lfb_ref, cg_ref, cvt_ref,
                 scr_ref):
    x = x_ref[...]
    ms = jnp.mean(x * x, axis=-1, keepdims=True)
    h = (x * lax.rsqrt(ms + EPS) * an_ref[...]).astype(BF16)

    cos = cos_ref[...]
    sin = sin_ref[...]
    nchunk = tm // C_CHUNK

    def norm_rope(y, yp, width, g, gp, scale):
        ones = ones_a_ref[0:width, 0:width]
        ss = _dot((y * y).astype(BF16), ones)
        r = lax.rsqrt(ss * (1.0 / A_HEAD_DIM) + EPS)
        out = (y * g * cos[:, 0:width] + yp * gp * sin[:, 0:width]) * r
        return out * scale if scale != 1.0 else out

    def norm_b(y, g, scale):
        ss = _dot((y * y).astype(BF16), ones_b_ref[...])
        r = lax.rsqrt(ss * (1.0 / B_HEAD_DIM) + EPS)
        out = y * g * r
        return out * scale if scale != 1.0 else out

    def with_ones_column(v):
        tail = (lax.broadcasted_iota(jnp.int32, (tm, LANES - v.shape[1]), 1) == 0).astype(F32)
        return jnp.concatenate([v, tail], axis=1).astype(BF16)

    def interleave(slot, val, out_ref):
        for c in range(C_WIDTH // LANES):
            scr_ref[slot, c] = val[:, c * LANES:(c + 1) * LANES]
        for t in range(C_CHUNK):
            for c in range(C_WIDTH // LANES):
                out_ref[0, t, :, c * LANES:(c + 1) * LANES] = scr_ref[slot, c, pl.ds(t, nchunk, stride=C_CHUNK), :]

    def forget_lower_bound():
        logits = lbl_ref[...]
        mx = jnp.max(logits, axis=0, keepdims=True)
        e = jnp.exp(logits - mx)
        p = e / jnp.sum(e, axis=0, keepdims=True)
        cum = p[0:1, :]
        for i in range(1, layer + 1):
            cum = cum + p[i:i + 1, :]
        return jnp.maximum(cum - p[0:1, :], 0.0)

    def epi_aq(y):
        q = norm_rope(y[:, 0:A_WIDTH], y[:, A_WIDTH:2 * A_WIDTH], A_WIDTH,
                      ga_ref[0:1, 0:A_WIDTH], ga_ref[1:2, 0:A_WIDTH], A_HEAD_DIM ** -0.5 * LOG2E)
        aq_ref[...] = q.astype(BF16)

    def epi_ak(y):
        k = norm_rope(y[:, 0:A_KV_WIDTH], y[:, A_KV_WIDTH:2 * A_KV_WIDTH], A_KV_WIDTH,
                      ga_ref[2:3, 0:A_KV_WIDTH], ga_ref[3:4, 0:A_KV_WIDTH], 1.0)
        akt_ref[...] = k.T.astype(BF16)

    def epi_av(y):
        for j in range(A_KV_HEADS):
            av_ref[j] = with_ones_column(y[:, j * A_HEAD_DIM:(j + 1) * A_HEAD_DIM])

    def epi_bq(y):
        bq_ref[...] = norm_b(y, gb_ref[0:1, :], B_HEAD_DIM ** -0.5 * LOG2E).astype(BF16)

    def epi_bk(y):
        bkt_ref[...] = norm_b(y, gb_ref[1:2, :], 1.0).T.astype(BF16)

    def epi_bv(y):
        for j in range(B_HEADS):
            bv_ref[j] = with_ones_column(y[:, j * B_V_DIM:(j + 1) * B_V_DIM])

    def epi_cq(y):
        interleave(0, y * _sigmoid(y), cq_ref)

    def epi_forget(slot, lf_ref):
        def epi(z):
            lb = forget_lower_bound()
            f = jnp.maximum(lb + (1.0 - lb) * _sigmoid(z), MIN_FORGET)
            interleave(slot, jnp.log(f), lf_ref)
        return epi

    def epi_cv(y):
        interleave(3, y, cv_ref)
        cvt_ref[...] = y.T.astype(BF16)

    def epi_cg(y):
        cg_ref[...] = (y * _sigmoid(y)).astype(BF16)

    oa_k = 2 * A_WIDTH
    oa_v = oa_k + 2 * A_KV_WIDTH
    groups = [
        (wa_ref, 0, 2 * A_WIDTH, epi_aq),
        (wa_ref, oa_k, 2 * A_KV_WIDTH, epi_ak),
        (wa_ref, oa_v, A_KV_WIDTH, epi_av),
        (wb_ref, 0, B_QK_WIDTH, epi_bq),
        (wb_ref, B_QK_WIDTH, B_QK_WIDTH, epi_bk),
        (wb_ref, 2 * B_QK_WIDTH, B_WIDTH, epi_bv),
        (wc_ref, 0, C_WIDTH, epi_cq),
        (wc_ref, C_WIDTH, C_WIDTH, epi_forget(1, lff_ref)),
        (wc_ref, 2 * C_WIDTH, C_WIDTH, epi_forget(2, lfb_ref)),
        (wc_ref, 3 * C_WIDTH, C_WIDTH, epi_cv),
        (wc_ref, 4 * C_WIDTH, C_WIDTH, epi_cg),
    ]
    pending = None
    for w_ref, lo, width, epi in groups:
        y = _dot(h, w_ref[:, lo:lo + width])
        if pending is not None:
            pending[0](pending[1])
        pending = (epi, y)
    pending[0](pending[1])


def _proj(layer, xf, an, wa, wb, wc, ga, gb, lbl, cos, sin, ones_a, ones_b, bsz, seq):
    tokens = bsz * seq
    tm = min(1024, seq)
    nt = seq // tm
    nchunk = tm // C_CHUNK
    const = lambda shape: pl.BlockSpec(shape, lambda i: (0,) * len(shape))
    tok = lambda w: pl.BlockSpec((tm, w), lambda i: (i, 0))
    tr = lambda w: pl.BlockSpec((w, tm), lambda i: (0, i))
    head = lambda nh, w: pl.BlockSpec((nh, tm, w), lambda i: (0, i, 0))
    inter = pl.BlockSpec((1, C_CHUNK, nchunk, C_WIDTH), lambda i: (i // nt, 0, i % nt, 0))
    inter_shape = jax.ShapeDtypeStruct((bsz, C_CHUNK, seq // C_CHUNK, C_WIDTH), F32)
    out_shape = (
        jax.ShapeDtypeStruct((tokens, A_WIDTH), BF16),
        jax.ShapeDtypeStruct((A_KV_WIDTH, tokens), BF16),
        jax.ShapeDtypeStruct((A_KV_HEADS, tokens, LANES), BF16),
        jax.ShapeDtypeStruct((tokens, B_QK_WIDTH), BF16),
        jax.ShapeDtypeStruct((B_QK_WIDTH, tokens), BF16),
        jax.ShapeDtypeStruct((B_HEADS, tokens, LANES), BF16),
        inter_shape, inter_shape, inter_shape, inter_shape,
        jax.ShapeDtypeStruct((tokens, C_WIDTH), BF16),
        jax.ShapeDtypeStruct((C_WIDTH, tokens), BF16),
    )
    out_specs = (
        tok(A_WIDTH), tr(A_KV_WIDTH), head(A_KV_HEADS, LANES),
        tok(B_QK_WIDTH), tr(B_QK_WIDTH), head(B_HEADS, LANES),
        inter, inter, inter, inter,
        tok(C_WIDTH), tr(C_WIDTH),
    )
    in_specs = [
        tok(D_MODEL), const((1, D_MODEL)),
        const(wa.shape), const(wb.shape), const(wc.shape),
        const(ga.shape), const(gb.shape), const(lbl.shape),
        pl.BlockSpec((tm, A_WIDTH), lambda i: (i % nt, 0)),
        pl.BlockSpec((tm, A_WIDTH), lambda i: (i % nt, 0)),
        const(ones_a.shape), const(ones_b.shape),
    ]
    return pl.pallas_call(
        functools.partial(_proj_kernel, layer, tm),
        grid=(tokens // tm,),
        in_specs=in_specs,
        out_specs=out_specs,
        out_shape=out_shape,
        scratch_shapes=[pltpu.VMEM((4, C_WIDTH // LANES, tm, LANES), F32)],
        compiler_params=pltpu.CompilerParams(
            dimension_semantics=("arbitrary",), vmem_limit_bytes=VMEM_LIMIT_BYTES),
        name=f"proj{layer}",
    )(xf, an, wa, wb, wc, ga, gb, lbl, cos, sin, ones_a, ones_b)


_L_MIN = 2.0 ** -100
_BOUND_SLACK = 1.02


def _score_bound(g_ref, dim):
    g = jnp.abs(g_ref[...])
    gq = jnp.max(g[0:1, :], axis=-1, keepdims=True)
    gk = jnp.max(g[1:2, :], axis=-1, keepdims=True)
    return gq * gk * (dim ** 0.5 * LOG2E * _BOUND_SLACK)


def _attn_a_kernel(tq, tk, q_ref, kt_ref, v_ref, g_ref, o_ref):
    seq = kt_ref.shape[1]
    nk = seq // tk
    bound = _score_bound(g_ref, A_HEAD_DIM)

    def group_q(g):
        heads = [A_GROUP * g + j for j in range(A_GROUP)]
        return heads, jnp.concatenate(
            [q_ref[:, h * A_HEAD_DIM:(h + 1) * A_HEAD_DIM] for h in heads], axis=0)

    def scores(qg, g, c):
        return _dot(qg, kt_ref[g * A_HEAD_DIM:(g + 1) * A_HEAD_DIM, c * tk:(c + 1) * tk])

    def store(g, heads, o):
        for j, h in enumerate(heads):
            o_ref[:, h * A_HEAD_DIM:(h + 1) * A_HEAD_DIM] = o[j * tq:(j + 1) * tq, 0:A_HEAD_DIM].astype(BF16)

    lmin = None
    for g in range(A_KV_HEADS):
        heads, qg = group_q(g)
        acc = None
        for c in range(nk):
            p = jnp.exp2(scores(qg, g, c) - bound).astype(BF16)
            pv = _dot(p, v_ref[g, c * tk:(c + 1) * tk, :])
            acc = pv if acc is None else acc + pv
        l = acc[:, A_HEAD_DIM:A_HEAD_DIM + 1]
        store(g, heads, acc * (1.0 / l))
        lg = jnp.min(l, axis=0, keepdims=True)
        lmin = lg if lmin is None else jnp.minimum(lmin, lg)

    @pl.when(jnp.logical_not(lmin[0, 0] >= _L_MIN))
    def _():
        for g in range(A_KV_HEADS):
            heads, qg = group_q(g)
            m = acc = None
            for c in range(nk):
                s = scores(qg, g, c)
                mc = jnp.max(s, axis=-1, keepdims=True)
                vc = v_ref[g, c * tk:(c + 1) * tk, :]
                if c == 0:
                    m = mc
                    acc = _dot(jnp.exp2(s - m).astype(BF16), vc)
                else:
                    mn = jnp.maximum(m, mc)
                    acc = jnp.exp2(m - mn) * acc + _dot(jnp.exp2(s - mn).astype(BF16), vc)
                    m = mn
            store(g, heads, acc * (1.0 / acc[:, A_HEAD_DIM:A_HEAD_DIM + 1]))


def _attn_a(layer, aq, akt, av, gains, bsz, seq):
    tokens = bsz * seq
    tq = min(512, seq)
    tk = min(256, seq)
    nq = seq // tq
    return pl.pallas_call(
        functools.partial(_attn_a_kernel, tq, tk),
        grid=(bsz, nq),
        in_specs=[
            pl.BlockSpec((tq, A_WIDTH), lambda b, i: (b * nq + i, 0)),
            pl.BlockSpec((A_KV_WIDTH, seq), lambda b, i: (0, b)),
            pl.BlockSpec((A_KV_HEADS, seq, LANES), lambda b, i: (0, b, 0)),
            pl.BlockSpec(gains.shape, lambda b, i: (0, 0)),
        ],
        out_specs=pl.BlockSpec((tq, A_WIDTH), lambda b, i: (b * nq + i, 0)),
        out_shape=jax.ShapeDtypeStruct((tokens, A_WIDTH), BF16),
        compiler_params=pltpu.CompilerParams(
            dimension_semantics=("arbitrary", "arbitrary"), vmem_limit_bytes=VMEM_LIMIT_BYTES),
        name=f"attn_a{layer}",
    )(aq, akt, av, gains)


def _attn_b_kernel(layer, tq, tk, nq, seq, q_ref, kt_ref, v_ref, dist_ref, lam_ref, gain_ref, g_ref, o_ref):
    i = pl.program_id(1)
    lam_p = lam_ref[...]
    lam_init = 0.8 - 0.6 * math.exp(-0.3 * layer)
    lam = (jnp.exp(jnp.sum(lam_p[0:1] * lam_p[1:2], axis=-1, keepdims=True))
           - jnp.exp(jnp.sum(lam_p[2:3] * lam_p[3:4], axis=-1, keepdims=True)) + lam_init)
    off = pl.multiple_of((nq - 1 - i) * tq, tq)
    nk = seq // tk
    bound = _score_bound(g_ref, B_HEAD_DIM)

    def scores(h, c, kc):
        lo = (2 * h + c) * B_HEAD_DIM
        slope = 2.0 ** (-8.0 * (h + 1) / B_HEADS)
        bias = dist_ref[:, pl.ds(off + kc * tk, tk)] * slope
        return _dot(q_ref[:, lo:lo + B_HEAD_DIM], kt_ref[lo:lo + B_HEAD_DIM, kc * tk:(kc + 1) * tk]) + bias

    def finish(h, outs):
        o = outs[0] - lam * outs[1]
        ms = jnp.mean(o * o, axis=-1, keepdims=True)
        o = o * lax.rsqrt(ms + EPS) * gain_ref[:, h * B_V_DIM:(h + 1) * B_V_DIM] * (1.0 - lam_init)
        o_ref[:, h * B_V_DIM:(h + 1) * B_V_DIM] = o.astype(BF16)

    lmin = None
    for h in range(B_HEADS):
        outs = []
        for c in range(2):
            acc = None
            for kc in range(nk):
                p = jnp.exp2(scores(h, c, kc) - bound).astype(BF16)
                pv = _dot(p, v_ref[h, kc * tk:(kc + 1) * tk, :])
                acc = pv if acc is None else acc + pv
            l = acc[:, B_V_DIM:B_V_DIM + 1]
            outs.append(acc[:, 0:B_V_DIM] * (1.0 / l))
            lg = jnp.min(l, axis=0, keepdims=True)
            lmin = lg if lmin is None else jnp.minimum(lmin, lg)
        finish(h, outs)

    @pl.when(jnp.logical_not(lmin[0, 0] >= _L_MIN))
    def _():
        for h in range(B_HEADS):
            outs = []
            for c in range(2):
                m = acc = None
                for kc in range(nk):
                    s = scores(h, c, kc)
                    mc = jnp.max(s, axis=-1, keepdims=True)
                    vc = v_ref[h, kc * tk:(kc + 1) * tk, :]
                    if kc == 0:
                        m = mc
                        acc = _dot(jnp.exp2(s - m).astype(BF16), vc)
                    else:
                        mn = jnp.maximum(m, mc)
                        acc = jnp.exp2(m - mn) * acc + _dot(jnp.exp2(s - mn).astype(BF16), vc)
                        m = mn
                outs.append(acc[:, 0:B_V_DIM] * (1.0 / acc[:, B_V_DIM:B_V_DIM + 1]))
            finish(h, outs)


def _attn_b_tiles(seq):
    return min(256, seq), min(1024, seq)


def _alibi_distance_table(seq):
    tq, _ = _attn_b_tiles(seq)
    r = jnp.arange(tq, dtype=jnp.int32)[:, None]
    u = jnp.arange(2 * seq - tq, dtype=jnp.int32)[None, :]
    return (-LOG2E) * jnp.abs(r - u + (seq - tq)).astype(F32)


def _attn_b(layer, bq, bkt, bv, dist, lam, gain, qk_gains, bsz, seq):
    tokens = bsz * seq
    tq, tk = _attn_b_tiles(seq)
    nq = seq // tq
    return pl.pallas_call(
        functools.partial(_attn_b_kernel, layer, tq, tk, nq, seq),
        grid=(bsz, nq),
        in_specs=[
            pl.BlockSpec((tq, B_QK_WIDTH), lambda b, i: (b * nq + i, 0)),
            pl.BlockSpec((B_QK_WIDTH, seq), lambda b, i: (0, b)),
            pl.BlockSpec((B_HEADS, seq, LANES), lambda b, i: (0, b, 0)),
            pl.BlockSpec(dist.shape, lambda b, i: (0, 0), pipeline_mode=pl.Buffered(1)),
            pl.BlockSpec(lam.shape, lambda b, i: (0, 0)),
            pl.BlockSpec(gain.shape, lambda b, i: (0, 0)),
            pl.BlockSpec(qk_gains.shape, lambda b, i: (0, 0)),
        ],
        out_specs=pl.BlockSpec((tq, B_WIDTH), lambda b, i: (b * nq + i, 0)),
        out_shape=jax.ShapeDtypeStruct((tokens, B_WIDTH), BF16),
        compiler_params=pltpu.CompilerParams(
            dimension_semantics=("arbitrary", "arbitrary"), vmem_limit_bytes=VMEM_LIMIT_BYTES),
        name=f"attn_b{layer}",
    )(bq, bkt, bv, dist, lam, gain, qk_gains)


_C_BLOCK = 16
_C_TILE = 128
_C_NPAIR = C_CHUNK * (C_CHUNK + 1) // 2
_C_PAIR_GROUPS = ((0, 8), (8, 11), (11, 14), (14, 16))


def _hgrn_kernel(seq, q_ref, v_ref, lff_ref, lfb_ref, vt_ref, g_ref, r_ref, gain_ref,
                 o_ref,
                 b_scr, k_scr, dec_scr, qn_scr, kn_scr, tt_scr, oi_scr, st_scr, kv_scr, ot_scr, on_scr):
    nchunk = seq // C_CHUNK
    ntile = seq // _C_TILE
    cpt = _C_TILE // C_CHUNK

    def same_head():
        rows_h = lax.broadcasted_iota(jnp.int32, (C_WIDTH, C_WIDTH), 0) // C_VAL_DIM
        cols_h = lax.broadcasted_iota(jnp.int32, (C_WIDTH, C_WIDTH), 1) // C_KEY_DIM
        return rows_h == cols_h

    nhalf = C_WIDTH // LANES

    def scatter(dst_ref, n0, t, val):
        for c in range(nhalf):
            dst_ref[c, pl.ds(n0 * C_CHUNK + t, val.shape[0], stride=C_CHUNK), :] = val[:, c * LANES:(c + 1) * LANES]

    def token_rows(src_ref, c0):
        return jnp.concatenate([src_ref[c, pl.ds(c0, _C_TILE), :] for c in range(nhalf)], axis=1)

    groups = _C_PAIR_GROUPS
    offs = [sum(j + 1 for j in range(groups[g][0])) for g in range(len(groups))]

    for d, lf_ref in enumerate((lff_ref, lfb_ref)):
        pos = (lambda j: j) if d == 0 else (lambda j: C_CHUNK - 1 - j)
        order = list(range(cpt)) if d == 0 else list(range(cpt - 1, -1, -1))
        tile_at = (lambda s: s) if d == 0 else (lambda s: ntile - 1 - s)

        def prepare(cb, carry):
            n0 = pl.multiple_of(cb * _C_BLOCK, _C_BLOCK)
            rows = pl.ds(n0, _C_BLOCK)
            b = None
            for j in range(C_CHUNK):
                lfj = lf_ref[0, pos(j), rows, :]
                k_scr[pos(j), rows, :] = 1.0 - jnp.exp(lfj)
                b = lfj if b is None else b + lfj
                b_scr[j, rows, :] = b
                scatter(qn_scr, n0, pos(j), q_ref[0, pos(j), rows, :] * jnp.exp(b))
            dec_scr[rows, :] = jnp.exp(b)
            for j in range(C_CHUNK):
                scatter(kn_scr, n0, pos(j), k_scr[pos(j), rows, :] * jnp.exp(b - b_scr[j, rows, :]))
            return carry

        lax.fori_loop(0, nchunk // _C_BLOCK, prepare, 0)

        def intra(cb, carry):
            rows = pl.ds(pl.multiple_of(cb * _C_BLOCK, _C_BLOCK), _C_BLOCK)

            def build(g):
                pi = offs[g]
                for j in range(*groups[g]):
                    bj = b_scr[j, rows, :]
                    qj = q_ref[0, pos(j), rows, :]
                    for i in range(j + 1):
                        w = jnp.exp(bj - b_scr[i, rows, :]) * qj * k_scr[pos(i), rows, :]
                        tt_scr[pi * _C_BLOCK:(pi + 1) * _C_BLOCK, :] = w.astype(BF16)
                        pi += 1
                return _dot(tt_scr[offs[g] * _C_BLOCK:pi * _C_BLOCK, :], r_ref[...])

            def consume(g, y):
                pi = 0
                for j in range(*groups[g]):
                    acc = None
                    for i in range(j + 1):
                        term = y[pi * _C_BLOCK:(pi + 1) * _C_BLOCK, :] * v_ref[0, pos(i), rows, :]
                        acc = term if acc is None else acc + term
                        pi += 1
                    if d == 0:
                        oi_scr[pos(j), rows, :] = acc
                    else:
                        oi_scr[pos(j), rows, :] = oi_scr[pos(j), rows, :] + acc

            prev = None
            for g in range(len(groups)):
                y = build(g)
                if prev is not None:
                    consume(*prev)
                prev = (g, y)
            consume(*prev)
            return carry

        lax.fori_loop(0, nchunk // _C_BLOCK, intra, 0)

        def kv_products(slot, tile):
            c0 = pl.multiple_of(tile * _C_TILE, _C_TILE)
            kt = token_rows(kn_scr, c0).astype(BF16)
            vt = vt_ref[:, pl.ds(c0, _C_TILE)]
            lane_chunk = lax.broadcasted_iota(jnp.int32, (C_WIDTH, _C_TILE), 1) // C_CHUNK
            vexp = jnp.concatenate(
                [jnp.where(lane_chunk == n, vt, jnp.zeros_like(vt)) for n in order], axis=0)
            kv_scr[slot] = _dot(vexp, kt)

        def chain(slot, tile):
            c0 = pl.multiple_of(tile * _C_TILE, _C_TILE)
            qt = token_rows(qn_scr, c0).astype(BF16)
            row_chunk = lax.broadcasted_iota(jnp.int32, (_C_TILE, C_WIDTH), 0) // C_CHUNK
            diag = same_head()
            st = st_scr[...]
            acc = None
            for nn, n in enumerate(order):
                qm = jnp.where(row_chunk == n, qt, jnp.zeros_like(qt))
                pv = _dot_nt(st.astype(BF16), qm)
                acc = pv if acc is None else acc + pv
                drow = dec_scr[pl.ds(tile * cpt + n, 1), :]
                kvn = kv_scr[slot, nn * C_WIDTH:(nn + 1) * C_WIDTH, :]
                st = st * drow + jnp.where(diag, kvn, 0.0)
            st_scr[...] = st
            ot_scr[d, :, pl.ds(c0, _C_TILE)] = acc

        st_scr[...] = jnp.zeros_like(st_scr)
        kv_products(0, tile_at(0))

        def inter(it, carry):
            s0 = 2 * it
            kv_products(1, tile_at(s0 + 1))
            chain(0, tile_at(s0))
            kv_products(0, tile_at(jnp.minimum(s0 + 2, ntile - 1)))
            chain(1, tile_at(s0 + 1))
            return carry

        lax.fori_loop(0, ntile // 2, inter, 0)

    def reorder(cb, carry):
        n0 = pl.multiple_of(cb * _C_BLOCK, _C_BLOCK)
        for t in range(C_CHUNK):
            scatter(on_scr, n0, t, oi_scr[t, pl.ds(n0, _C_BLOCK), :])
        return carry

    lax.fori_loop(0, nchunk // _C_BLOCK, reorder, 0)

    def finish(ti, carry):
        c0 = pl.multiple_of(ti * _C_TILE, _C_TILE)
        o = token_rows(on_scr, c0) + (ot_scr[0, :, pl.ds(c0, _C_TILE)] + ot_scr[1, :, pl.ds(c0, _C_TILE)]).T
        ss = _dot((o * o).astype(BF16), r_ref[...])
        o = o * lax.rsqrt(ss * (1.0 / C_VAL_DIM) + EPS) * gain_ref[...]
        o_ref[pl.ds(c0, _C_TILE), :] = (o * g_ref[pl.ds(c0, _C_TILE), :].astype(F32)).astype(BF16)
        return carry

    lax.fori_loop(0, ntile, finish, 0)


def _hgrn(layer, cq, cv, lff, lfb, cvt, cg, ones_c, gain, bsz, seq):
    tokens = bsz * seq
    nchunk = seq // C_CHUNK
    inter = pl.BlockSpec((1, C_CHUNK, nchunk, C_WIDTH), lambda b: (b, 0, 0, 0))
    return pl.pallas_call(
        functools.partial(_hgrn_kernel, seq),
        grid=(bsz,),
        in_specs=[
            inter, inter, inter, inter,
            pl.BlockSpec((C_WIDTH, seq), lambda b: (0, b)),
            pl.BlockSpec((seq, C_WIDTH), lambda b: (b, 0)),
            pl.BlockSpec((C_WIDTH, C_WIDTH), lambda b: (0, 0)),
            pl.BlockSpec((1, C_WIDTH), lambda b: (0, 0)),
        ],
        out_specs=pl.BlockSpec((seq, C_WIDTH), lambda b: (b, 0)),
        out_shape=jax.ShapeDtypeStruct((tokens, C_WIDTH), BF16),
        scratch_shapes=[
            pltpu.VMEM((C_CHUNK, nchunk, C_WIDTH), F32),
            pltpu.VMEM((C_CHUNK, nchunk, C_WIDTH), F32),
            pltpu.VMEM((nchunk, C_WIDTH), F32),
            pltpu.VMEM((C_WIDTH // LANES, seq, LANES), F32),
            pltpu.VMEM((C_WIDTH // LANES, seq, LANES), F32),
            pltpu.VMEM((_C_NPAIR * _C_BLOCK, C_WIDTH), BF16),
            pltpu.VMEM((C_CHUNK, nchunk, C_WIDTH), F32),
            pltpu.VMEM((C_WIDTH, C_WIDTH), F32),
            pltpu.VMEM((2, (_C_TILE // C_CHUNK) * C_WIDTH, C_WIDTH), F32),
            pltpu.VMEM((2, C_WIDTH, seq), F32),
            pltpu.VMEM((C_WIDTH // LANES, seq, LANES), F32),
        ],
        compiler_params=pltpu.CompilerParams(
            dimension_semantics=("arbitrary",), vmem_limit_bytes=VMEM_LIMIT_BYTES),
        name=f"hgrn{layer}",
    )(cq, cv, lff, lfb, cvt, cg, ones_c, gain)


_F_BLOCK = 256
_F_ROWS = 256
_F_OUT_ROWS = 512


def _gelu(x):
    return 0.5 * x * (1.0 + lax.erf(x * (2.0 ** -0.5)))


def _ffn_kernel(seq, nsteps, x_hbm, oa_hbm, ob_hbm, oc_hbm, wo_ref, fn_ref,
                wgf_ref, wvf_ref, wg0_ref, wv0_ref, wg1_ref, wv1_ref,
                cw0_ref, cb0_ref, wd0_ref, cw1_ref, cb1_ref, wd1_ref, cwl_ref, cbl_ref, wdl_ref,
                o_ref, x_ref, oa_ref, ob_ref, oc_ref, in_sem, h_scr, g_scr, v_scr):
    b = pl.program_id(0)
    i = pl.program_id(1)
    rb = min(_F_ROWS, seq)
    halo = 8

    def input_copies(batch):
        rows = pl.ds(pl.multiple_of(batch * seq, seq), seq)
        pairs = ((x_hbm, x_ref), (oa_hbm, oa_ref), (ob_hbm, ob_ref), (oc_hbm, oc_ref))
        return [pltpu.make_async_copy(src.at[rows, :], dst, in_sem.at[n]) for n, (src, dst) in enumerate(pairs)]

    @pl.when(jnp.logical_and(b == 0, i == 0))
    def _():
        for cp in input_copies(b):
            cp.start()

    @pl.when(i == 0)
    def _():
        for cp in input_copies(b):
            cp.wait()

    def up(slot, r, wg_ref, wv_ref):
        rows = slice(r * rb, (r + 1) * rb)
        h = h_scr[rows, :]
        g_scr[slot, rows, :] = _dot(h, wg_ref[...])
        v_scr[slot, rows, :] = _dot(h, wv_ref[...])

    def consume(slot, r, cw_ref, cb_ref, wd_ref):
        lo = r * rb
        a, b = max(lo - halo, 0), min(lo + rb + halo, seq)
        n, off = b - a, lo - a
        g = g_scr[slot, a:b, :]
        row = lax.broadcasted_iota(jnp.int32, g.shape, 0) + a
        prev = jnp.where(row == 0, 0.0, pltpu.roll(g, 1, 0))[off:off + rb]
        nxt = jnp.where(row == seq - 1, 0.0, pltpu.roll(g, n - 1, 0))[off:off + rb]
        cw = cw_ref[...]
        conv = prev * cw[0:1, :] + g[off:off + rb] * cw[1:2, :] + nxt * cw[2:3, :] + cb_ref[...]
        act = (_gelu(conv) * v_scr[slot, lo:lo + rb, :]).astype(BF16)
        o_ref[lo:lo + rb, :] += _dot(act, wd_ref[...])

    nrb = seq // rb

    @pl.when(i == 0)
    def _():
        prb = max(rb, min(_F_OUT_ROWS, seq))
        for r in range(seq // prb):
            rows = slice(r * prb, (r + 1) * prb)
            mix = jnp.concatenate([oa_ref[rows, :], ob_ref[rows, :], oc_ref[rows, :]], axis=1)
            xm = x_ref[rows, :] + _dot(mix, wo_ref[...])
            o_ref[rows, :] = xm
            ms = jnp.mean(xm * xm, axis=-1, keepdims=True)
            h_scr[rows, :] = (xm * lax.rsqrt(ms + EPS) * fn_ref[...]).astype(BF16)
            for rr in range(r * (prb // rb), (r + 1) * (prb // rb)):
                up(0, rr, wgf_ref, wvf_ref)

    @pl.when(jnp.logical_and(i == 1, b + 1 < pl.num_programs(0)))
    def _():
        for cp in input_copies(b + 1):
            cp.start()

    for r in range(nrb):
        consume(0, r, cw0_ref, cb0_ref, wd0_ref)
        up(1, r, wg0_ref, wv0_ref)
    for r in range(nrb):
        consume(1, r, cw1_ref, cb1_ref, wd1_ref)
        up(0, r, wg1_ref, wv1_ref)

    @pl.when(i == nsteps - 1)
    def _():
        for r in range(nrb):
            consume(0, r, cwl_ref, cbl_ref, wdl_ref)


def _ffn(layer, xf, oa, ob, oc, wo, fn, wup, cw, cb, wd, bsz, seq):
    tokens = bsz * seq
    nf = D_FF // _F_BLOCK
    assert D_FF % _F_BLOCK == 0 and nf % 2 == 1
    nsteps = (nf - 1) // 2
    assert nsteps >= 2
    single = pl.Buffered(1)
    up_spec = lambda blk: [pl.BlockSpec((D_MODEL, _F_BLOCK), lambda b, i: (0, blk(i))),
                           pl.BlockSpec((D_MODEL, _F_BLOCK), lambda b, i: (0, blk(i) + nf))]
    down_spec = lambda blk: [pl.BlockSpec((3, _F_BLOCK), lambda b, i: (0, blk(i))),
                             pl.BlockSpec((1, _F_BLOCK), lambda b, i: (0, blk(i))),
                             pl.BlockSpec((_F_BLOCK, D_MODEL), lambda b, i: (blk(i), 0))]
    return pl.pallas_call(
        functools.partial(_ffn_kernel, seq, nsteps),
        grid=(bsz, nsteps),
        in_specs=[
            pl.BlockSpec(memory_space=pl.ANY),
            pl.BlockSpec(memory_space=pl.ANY),
            pl.BlockSpec(memory_space=pl.ANY),
            pl.BlockSpec(memory_space=pl.ANY),
            pl.BlockSpec((D_MIX, D_MODEL), lambda b, i: (0, 0), pipeline_mode=single),
            pl.BlockSpec((1, D_MODEL), lambda b, i: (0, 0)),
            *up_spec(lambda i: 0), *up_spec(lambda i: 2 * i + 1), *up_spec(lambda i: 2 * i + 2),
            *down_spec(lambda i: 2 * i), *down_spec(lambda i: 2 * i + 1), *down_spec(lambda i: nf - 1),
        ],
        out_specs=pl.BlockSpec((seq, D_MODEL), lambda b, i: (b, 0)),
        out_shape=jax.ShapeDtypeStruct((tokens, D_MODEL), F32),
        scratch_shapes=[
            pltpu.VMEM((seq, D_MODEL), F32),
            pltpu.VMEM((seq, A_WIDTH), BF16),
            pltpu.VMEM((seq, B_WIDTH), BF16),
            pltpu.VMEM((seq, C_WIDTH), BF16),
            pltpu.SemaphoreType.DMA((4,)),
            pltpu.VMEM((seq, D_MODEL), BF16),
            pltpu.VMEM((2, seq, _F_BLOCK), F32),
            pltpu.VMEM((2, seq, _F_BLOCK), F32),
        ],
        compiler_params=pltpu.CompilerParams(
            dimension_semantics=("arbitrary", "arbitrary"), vmem_limit_bytes=VMEM_LIMIT_BYTES),
        name=f"ffn{layer}",
    )(xf, oa, ob, oc, wo, fn, wup, wup, wup, wup, wup, wup, cw, cb, wd, cw, cb, wd, cw, cb, wd)


def _rope_tables(seq):
    rows = seq // GRID_W
    row = jnp.repeat(jnp.arange(rows), GRID_W)
    col = jnp.tile(jnp.arange(GRID_W), rows)
    inv_freq = ROPE_THETA ** (-jnp.arange(0, A_AXIS_DIM, 2, dtype=F32) / A_AXIS_DIM)
    ang = jnp.stack([row, col], axis=-1).astype(F32)[..., None] * inv_freq
    cos, sin = jnp.cos(ang), jnp.sin(ang)
    cos_h = jnp.concatenate([cos[:, 0], cos[:, 0], cos[:, 1], cos[:, 1]], axis=-1)
    sin_h = jnp.concatenate([-sin[:, 0], sin[:, 0], -sin[:, 1], sin[:, 1]], axis=-1)
    return jnp.tile(cos_h, (1, A_HEADS)), jnp.tile(sin_h, (1, A_HEADS))


def _partner_index(width):
    half = A_AXIS_DIM // 2
    i = np.arange(width)
    return np.where((i % A_AXIS_DIM) < half, i + half, i - half)


def _block_ones(width, block):
    i = np.arange(width) // block
    return jnp.asarray(i[:, None] == i[None, :], BF16)


def kernel(x, attn_norm, w_in, a_q_norm, a_k_norm, b_q_norm, b_k_norm, b_lambda, b_sub_norm,
           c_lb_logits, c_out_norm, w_out, ffn_norm, w_up, conv_w, conv_b, w_down):
    bsz, seq, _ = x.shape
    tokens = bsz * seq
    assert seq % 256 == 0 and seq % GRID_W == 0

    cos, sin = _rope_tables(seq)
    ones_a = _block_ones(A_WIDTH, A_HEAD_DIM)
    ones_b = _block_ones(B_QK_WIDTH, B_HEAD_DIM)
    ones_c = _block_ones(C_WIDTH, C_VAL_DIM)
    pq = _partner_index(A_WIDTH)
    pk = _partner_index(A_KV_WIDTH)

    dist = _alibi_distance_table(seq)

    xf = x.reshape(tokens, D_MODEL)
    for l in range(DEPTH):
        w = w_in[l]
        wq, wk, wv = (w[:, _OFF[0]:_OFF[1]], w[:, _OFF[1]:_OFF[2]], w[:, _OFF[2]:_OFF[3]])
        wa = jnp.concatenate([wq, wq[:, pq], wk, wk[:, pk], wv], axis=1).astype(BF16)
        wb = w[:, _OFF[3]:_OFF[6]].astype(BF16)
        wc = w[:, _OFF[6]:_OFF[11]].astype(BF16)
        gq = jnp.tile(a_q_norm[l], A_HEADS)
        gk = jnp.pad(jnp.tile(a_k_norm[l], A_KV_HEADS), (0, A_WIDTH - A_KV_WIDTH))
        gkp = jnp.pad(jnp.tile(a_k_norm[l], A_KV_HEADS)[pk], (0, A_WIDTH - A_KV_WIDTH))
        ga = jnp.stack([gq, gq[pq], gk, gkp])
        gb = jnp.stack([jnp.tile(b_q_norm[l], 2 * B_HEADS), jnp.tile(b_k_norm[l], 2 * B_HEADS)])

        (aq, akt, av, bq, bkt, bv, cq, cv, lff, lfb, cg, cvt) = _proj(
            l, xf, attn_norm[l][None, :], wa, wb, wc, ga, gb, c_lb_logits, cos, sin, ones_a, ones_b,
            bsz, seq)
        oa = _attn_a(l, aq, akt, av, jnp.stack([a_q_norm[l], a_k_norm[l]]), bsz, seq)
        ob = _attn_b(l, bq, bkt, bv, dist, b_lambda[l], b_sub_norm[l][None, :],
                     jnp.stack([b_q_norm[l], b_k_norm[l]]), bsz, seq)
        oc = _hgrn(l, cq, cv, lff, lfb, cvt, cg, ones_c, c_out_norm[l][None, :], bsz, seq)
        xf = _ffn(l, xf, oa, ob, oc, w_out[l].astype(BF16), ffn_norm[l][None, :],
                  w_up[l].astype(BF16), conv_w[l], conv_b[l][None, :], w_down[l].astype(BF16), bsz, seq)
    return xf.reshape(bsz, seq, D_MODEL)
```
